```python
import jax, jax.numpy as jnp
from jax import lax
import numpy as np

D_MODEL = 2048
BATCH = 4
SEQ = 2048
DEPTH = 2

GRID_W = 64
CTX_LEN = 256
HEAD_DIM = 128
NA_HEADS = D_MODEL // (2 * HEAD_DIM)
GQA_HEADS = D_MODEL // (2 * HEAD_DIM)
GQA_KV_HEADS = GQA_HEADS // 4
NA_KH = 8
NA_KW = 16
Q_BLOCK = 128
ROPE_THETA = 10000.0
MLSTM_HEADS = 8
MLSTM_QK_DIM = D_MODEL // (2 * MLSTM_HEADS)
MLSTM_V_DIM = D_MODEL // MLSTM_HEADS
MLSTM_CHUNK = 64
GATE_SOFTCAP = 15.0
N_EXPERTS = 32
TOP_K = 4
D_EXPERT = D_MODEL
SWIGLU_ALPHA = 1.702
SWIGLU_LIMIT = 7.0
MOE_BLOCK = 256
NORM_EPS = 1e-6
N_ATTN_LAYERS = (DEPTH + 1) // 2
N_MLSTM_LAYERS = DEPTH // 2
ATTN_SPLITS = (NA_HEADS * HEAD_DIM, NA_HEADS * HEAD_DIM, NA_HEADS * HEAD_DIM,
               GQA_HEADS * HEAD_DIM, GQA_KV_HEADS * HEAD_DIM, GQA_KV_HEADS * HEAD_DIM)
ATTN_IN_WIDTH = sum(ATTN_SPLITS)
ATTN_OUT_WIDTH = (NA_HEADS + GQA_HEADS) * HEAD_DIM
MLSTM_SPLITS = (MLSTM_HEADS * MLSTM_QK_DIM, MLSTM_HEADS * MLSTM_QK_DIM, MLSTM_HEADS * MLSTM_V_DIM,
                MLSTM_HEADS * MLSTM_V_DIM, 4 * MLSTM_HEADS)
MLSTM_IN_WIDTH = sum(MLSTM_SPLITS)

kernel_name = 'hybrid_na_gqa_mlstm_moe_dit'


def rms_norm(x, g):
    x32 = x.astype(jnp.float32)
    y = x32 * lax.rsqrt(jnp.mean(x32 * x32, axis=-1, keepdims=True) + NORM_EPS)
    return (y * g.astype(jnp.float32)).astype(x.dtype)


def modulate(x, g, shift, scale):
    return rms_norm(x, g) * (1 + scale) + shift


def split_cols(t, sizes):
    return jnp.split(t, np.cumsum(sizes)[:-1].tolist(), axis=-1)


def split_heads(t, n_heads):
    B, N, _ = t.shape
    return t.reshape(B, N, n_heads, -1).transpose(0, 2, 1, 3)


def merge_heads(t):
    B, H, N, Dh = t.shape
    return t.transpose(0, 2, 1, 3).reshape(B, N, H * Dh)


def softmax_f32(s):
    return jax.nn.softmax(s.astype(jnp.float32), axis=-1)


def axial_rope(n_tokens):
    t = jnp.arange(n_tokens)
    row = (t // GRID_W).astype(jnp.float32)
    col = (t % GRID_W).astype(jnp.float32)
    axis_dim = HEAD_DIM // 2
    inv_freq = ROPE_THETA ** (-jnp.arange(0, axis_dim, 2, dtype=jnp.float32) / axis_dim)
    ang = jnp.concatenate([row[:, None] * inv_freq, col[:, None] * inv_freq], axis=-1)
    return jnp.cos(ang), jnp.sin(ang)


def apply_rope(x, cos, sin):
    xp = x.astype(jnp.float32).reshape(x.shape[:-1] + (-1, 2))
    xe, xo = xp[..., 0], xp[..., 1]
    out = jnp.stack([xe * cos - xo * sin, xe * sin + xo * cos], axis=-1)
    return out.reshape(x.shape).astype(x.dtype)


def gqa_attention(q, k, v):
    B, Hq, Nq, Dh = q.shape
    Hkv = k.shape[1]
    G = Hq // Hkv
    nb = Nq // Q_BLOCK
    scale = Dh ** -0.5
    qb = q.reshape(B, Hkv, G, nb, Q_BLOCK, Dh).transpose(3, 0, 1, 2, 4, 5)

    def block(qi):
        s = jnp.einsum('bkgqd,bknd->bkgqn', qi, k) * scale
        p = softmax_f32(s).astype(v.dtype)
        return jnp.einsum('bkgqn,bknd->bkgqd', p, v)

    o = lax.map(block, qb)
    return o.transpose(1, 2, 3, 0, 4, 5).reshape(B, Hq, Nq, Dh)


def neighborhood_attention(q, k, v, k_ctx, v_ctx, rpb):
    B, H, S, Dh = q.shape
    rows = S // GRID_W
    kh = min(NA_KH, rows)
    n_win = kh * NA_KW
    scale = Dh ** -0.5
    kg = k.reshape(B, H, rows, GRID_W, Dh)
    vg = v.reshape(B, H, rows, GRID_W, Dh)
    q_rows = q.reshape(B, H, rows, GRID_W, Dh).transpose(2, 0, 1, 3, 4)
    cols = jnp.arange(GRID_W)
    col_idx = jnp.clip(cols - NA_KW // 2, 0, GRID_W - NA_KW)[:, None] + jnp.arange(NA_KW)
    col_rel = col_idx - cols[:, None] + NA_KW - 1
    row_ids = jnp.arange(rows)
    row_start = jnp.clip(row_ids - kh // 2, 0, rows - kh)

    def row_block(args):
        r, r0, q_blk = args
        k_win = lax.dynamic_slice_in_dim(kg, r0, kh, axis=2)[:, :, :, col_idx]
        v_win = lax.dynamic_slice_in_dim(vg, r0, kh, axis=2)[:, :, :, col_idx]
        row_rel = r0 + jnp.arange(kh) - r + NA_KH - 1
        bias = rpb[:, row_rel][:, :, col_rel]
        s_win = jnp.einsum('bhqd,bhrqwd->bhqrw', q_blk, k_win) * scale + bias.transpose(0, 2, 1, 3)
        s_ctx = jnp.einsum('bhqd,bhcd->bhqc', q_blk, k_ctx) * scale
        s = jnp.concatenate([s_win.reshape(B, H, GRID_W, n_win), s_ctx], axis=-1)
        p = softmax_f32(s).astype(v.dtype)
        p_win = p[..., :n_win].reshape(B, H, GRID_W, kh, NA_KW)
        return (jnp.einsum('bhqrw,bhrqwd->bhqd', p_win, v_win)
                + jnp.einsum('bhqc,bhcd->bhqd', p[..., n_win:], v_ctx))

    o = lax.map(row_block, (row_ids, row_start, q_rows))
    return o.transpose(1, 2, 0, 3, 4).reshape(B, H, S, Dh)


def hybrid_attention(h_lat, h_ctx, w_in, w_out, qk_gain, rpb, with_ctx_out):
    S = h_lat.shape[1]
    qa, ka, va, qb, kb, vb = split_cols(h_lat @ w_in, ATTN_SPLITS)
    qa_c, ka_c, va_c, qb_c, kb_c, vb_c = split_cols(h_ctx @ w_in, ATTN_SPLITS)
    qa, qa_c = [rms_norm(split_heads(t, NA_HEADS), qk_gain[0]) for t in (qa, qa_c)]
    ka, ka_c = [rms_norm(split_heads(t, NA_HEADS), qk_gain[1]) for t in (ka, ka_c)]
    va, va_c = [split_heads(t, NA_HEADS) for t in (va, va_c)]
    qb, qb_c = [rms_norm(split_heads(t, GQA_HEADS), qk_gain[2]) for t in (qb, qb_c)]
    kb, kb_c = [rms_norm(split_heads(t, GQA_KV_HEADS), qk_gain[3]) for t in (kb, kb_c)]
    vb, vb_c = [split_heads(t, GQA_KV_HEADS) for t in (vb, vb_c)]
    cos, sin = axial_rope(S)
    qb = apply_rope(qb, cos, sin)
    kb = apply_rope(kb, cos, sin)
    o_na = neighborhood_attention(qa, ka, va, ka_c, va_c, rpb)
    o_gqa = gqa_attention(qb, jnp.concatenate([kb_c, kb], axis=2), jnp.concatenate([vb_c, vb], axis=2))
    y_lat = jnp.concatenate([merge_heads(o_na), merge_heads(o_gqa)], axis=-1) @ w_out
    if not with_ctx_out:
        return y_lat, None
    o_na_c = gqa_attention(qa_c, ka_c, va_c)
    o_gqa_c = gqa_attention(qb_c, kb_c, vb_c)
    y_ctx = jnp.concatenate([merge_heads(o_na_c), merge_heads(o_gqa_c)], axis=-1) @ w_out
    return y_lat, y_ctx


def mlstm_chunkwise(q, k, v, log_i, log_f, state, with_output):
    B, H, N, _ = q.shape
    L = MLSTM_CHUNK
    nc = N // L

    def chunks(t):
        return jnp.moveaxis(t.reshape((B, H, nc, L) + t.shape[3:]), 2, 0)

    causal = jnp.tril(jnp.ones((L, L), dtype=bool))

    def step(carry, inp):
        C, n, m = carry
        qc, kc, vc, ic, fc = inp
        b = jnp.cumsum(fc, axis=-1)
        b_last = b[..., -1]
        g = b_last[..., None] - b + ic
        m_new = jnp.maximum(b_last + m, g.max(-1))
        decay = jnp.exp(b_last + m - m_new)
        wk = jnp.exp(g - m_new[..., None])
        C_new = decay[..., None, None] * C + jnp.einsum('bhs,bhsd,bhsv->bhdv', wk, kc, vc)
        n_new = decay[..., None] * n + jnp.einsum('bhs,bhsd->bhd', wk, kc)
        if not with_output:
            return (C_new, n_new, m_new), None
        d = jnp.where(causal, b[..., :, None] - b[..., None, :] + ic[..., None, :], -jnp.inf)
        inter = b + m[..., None]
        m_t = jnp.maximum(inter, d.max(-1))
        a = jnp.exp(inter - m_t)
        s = jnp.einsum('bhtd,bhsd->bhts', qc, kc) * jnp.exp(d - m_t[..., None])
        num = a[..., None] * jnp.einsum('bhtd,bhdv->bhtv', qc, C) + jnp.einsum('bhts,bhsv->bhtv', s, vc)
        den = a * jnp.einsum('bhtd,bhd->bht', qc, n) + s.sum(-1)
        h = num / jnp.maximum(jnp.abs(den), jnp.exp(-m_t))[..., None]
        return (C_new, n_new, m_new), h

    state, h = lax.scan(step, state, (chunks(q), chunks(k), chunks(v), chunks(log_i), chunks(log_f)))
    if with_output:
        h = jnp.moveaxis(h, 0, 2).reshape(B, H, N, -1)
    return h, state


def bidirectional_mlstm(h_lat, h_ctx, w_in, gate_bias, head_gain, w_out, with_ctx_out):
    def project(h):
        B, N, _ = h.shape
        q, k, v, o, g = split_cols(h @ w_in, MLSTM_SPLITS)
        q = split_heads(q, MLSTM_HEADS).astype(jnp.float32) * MLSTM_QK_DIM ** -0.5
        k = split_heads(k, MLSTM_HEADS).astype(jnp.float32)
        v = split_heads(v, MLSTM_HEADS).astype(jnp.float32)
        g = g.astype(jnp.float32) + gate_bias.reshape(-1).astype(jnp.float32)
        g = (GATE_SOFTCAP * jnp.tanh(g / GATE_SOFTCAP)).reshape(B, N, 4, MLSTM_HEADS).transpose(2, 0, 3, 1)
        return q, k, v, o, g[0::2], jax.nn.log_sigmoid(g[1::2])

    def readout(h, o):
        hn = merge_heads(rms_norm(h, head_gain[:, None, :]))
        return (hn.astype(o.dtype) * jax.nn.sigmoid(o)) @ w_out

    def flip(t):
        return jnp.flip(t, axis=2)

    ql, kl, vl, ol, il, fl = project(h_lat)
    qc, kc, vc, oc, ic, fc = project(h_ctx)
    B = h_lat.shape[0]
    zero = (jnp.zeros((B, MLSTM_HEADS, MLSTM_QK_DIM, MLSTM_V_DIM), jnp.float32),
            jnp.zeros((B, MLSTM_HEADS, MLSTM_QK_DIM), jnp.float32),
            jnp.zeros((B, MLSTM_HEADS), jnp.float32))
    hc_f, st_f = mlstm_chunkwise(qc, kc, vc, ic[0], fc[0], zero, with_ctx_out)
    hl_f, _ = mlstm_chunkwise(ql, kl, vl, il[0], fl[0], st_f, True)
    hc_b, st_b = mlstm_chunkwise(flip(qc), flip(kc), flip(vc), flip(ic[1]), flip(fc[1]), zero, with_ctx_out)
    hl_b, _ = mlstm_chunkwise(flip(ql), flip(kl), flip(vl), flip(il[1]), flip(fl[1]), st_b, True)
    y_lat = readout(hl_f + flip(hl_b), ol)
    if not with_ctx_out:
        return y_lat, None
    return y_lat, readout(hc_f + flip(hc_b), oc)


def clamped_swiglu(hcat):
    x_glu = jnp.minimum(hcat[..., ::2], SWIGLU_LIMIT)
    x_lin = jnp.clip(hcat[..., 1::2], -SWIGLU_LIMIT, SWIGLU_LIMIT)
    return x_glu * jax.nn.sigmoid(SWIGLU_ALPHA * x_glu) * (x_lin + 1)


def moe_ffn(h, router_w, router_b, w1, b1, w2, b2):
    T, D = h.shape
    M = T * TOP_K
    logits = (h @ router_w + router_b).astype(jnp.float32)
    top_logit, top_e = lax.top_k(logits, TOP_K)
    gate = jax.nn.softmax(top_logit, axis=-1).astype(h.dtype).reshape(M)
    e_flat = top_e.reshape(M)
    order = jnp.argsort(e_flat)
    e_sorted = e_flat[order]
    tok_sorted = order // TOP_K
    gate_sorted = gate[order]
    counts = jnp.bincount(e_flat, length=N_EXPERTS)
    padded = (counts + MOE_BLOCK - 1) // MOE_BLOCK * MOE_BLOCK
    pad_end = jnp.cumsum(padded)
    pad_start = pad_end - padded
    sort_start = jnp.cumsum(counts) - counts
    dest = pad_start[e_sorted] + jnp.arange(M) - sort_start[e_sorted]
    n_blocks = -(-(M + N_EXPERTS * (MOE_BLOCK - 1)) // MOE_BLOCK)
    rows = jnp.zeros((n_blocks * MOE_BLOCK, D), h.dtype).at[dest].set(h[tok_sorted])
    block_expert = jnp.minimum(jnp.searchsorted(pad_end, jnp.arange(n_blocks) * MOE_BLOCK, side='right'),
                               N_EXPERTS - 1)

    def expert_block(args):
        xb, e = args
        return clamped_swiglu(xb @ w1[e] + b1[e]) @ w2[e] + b2[e]

    y_rows = lax.map(expert_block, (rows.reshape(n_blocks, MOE_BLOCK, D), block_expert)).reshape(-1, D)
    return jnp.zeros((T, D), h.dtype).at[tok_sorted].add(y_rows[dest] * gate_sorted[:, None])


def setup_inputs(seed: int = 0) -> dict:
    key = jax.random.key(seed)
    ks = jax.random.split(key, 21)
    f32 = jnp.float32

    def nrm(k, shape, scale):
        return jax.random.normal(k, shape, f32) * scale

    gate_offset = jnp.array([0.0, 3.0, 0.0, 3.0], f32)[None, :, None]
    return {
        'x': nrm(ks[0], (BATCH, SEQ, D_MODEL), 1.0),
        'c': nrm(ks[1], (BATCH, D_MODEL), 1.0),
        'ctx': nrm(ks[2], (BATCH, CTX_LEN, D_MODEL), 1.0),
        'c_ctx': nrm(ks[3], (D_MODEL,), 1.0),
        'ada_w': nrm(ks[4], (DEPTH, D_MODEL, 6 * D_MODEL), 0.5 * D_MODEL ** -0.5),
        'ada_b': nrm(ks[5], (DEPTH, 6 * D_MODEL), 0.02),
        'norm_g': 1.0 + nrm(ks[6], (DEPTH, 2, D_MODEL), 0.02),
        'attn_w_in': nrm(ks[7], (N_ATTN_LAYERS, D_MODEL, ATTN_IN_WIDTH), D_MODEL ** -0.5),
        'attn_w_out': nrm(ks[8], (N_ATTN_LAYERS, ATTN_OUT_WIDTH, D_MODEL), ATTN_OUT_WIDTH ** -0.5),
        'attn_qk_gain': 1.0 + nrm(ks[9], (N_ATTN_LAYERS, 4, HEAD_DIM), 0.02),
        'na_rpb': nrm(ks[10], (N_ATTN_LAYERS, NA_HEADS, 2 * NA_KH - 1, 2 * NA_KW - 1), 0.1),
        'mlstm_w_in': nrm(ks[11], (N_MLSTM_LAYERS, D_MODEL, MLSTM_IN_WIDTH), D_MODEL ** -0.5),
        'mlstm_gate_bias': nrm(ks[12], (N_MLSTM_LAYERS, 4, MLSTM_HEADS), 0.5) + gate_offset,
        'mlstm_head_gain': 1.0 + nrm(ks[13], (N_MLSTM_LAYERS, MLSTM_HEADS, MLSTM_V_DIM), 0.02),
        'mlstm_w_out': nrm(ks[14], (N_MLSTM_LAYERS, MLSTM_HEADS * MLSTM_V_DIM, D_MODEL),
                           (MLSTM_HEADS * MLSTM_V_DIM) ** -0.5),
        'router_w': nrm(ks[15], (DEPTH, D_MODEL, N_EXPERTS), D_MODEL ** -0.5),
        'router_b': nrm(ks[16], (DEPTH, N_EXPERTS), 0.01),
        'expert_w1': nrm(ks[17], (DEPTH, N_EXPERTS, D_MODEL, 2 * D_EXPERT), D_MODEL ** -0.5),
        'expert_b1': nrm(ks[18], (DEPTH, N_EXPERTS, 2 * D_EXPERT), 0.02),
        'expert_w2': nrm(ks[19], (DEPTH, N_EXPERTS, D_EXPERT, D_MODEL), D_EXPERT ** -0.5),
        'expert_b2': nrm(ks[20], (DEPTH, N_EXPERTS, D_MODEL), 0.02),
    }


def reference(x, c, ctx, c_ctx, ada_w, ada_b, norm_g, attn_w_in, attn_w_out, attn_qk_gain, na_rpb,
              mlstm_w_in, mlstm_gate_bias, mlstm_head_gain, mlstm_w_out,
              router_w, router_b, expert_w1, expert_b1, expert_w2, expert_b2):
    B, S, D = x.shape
    Lc = ctx.shape[1]
    for layer in range(DEPTH):
        last = layer == DEPTH - 1
        mod_lat = jnp.split((jax.nn.silu(c) @ ada_w[layer] + ada_b[layer])[:, None, :], 6, axis=-1)
        mod_ctx = jnp.split((jax.nn.silu(c_ctx) @ ada_w[layer] + ada_b[layer])[None, None, :], 6, axis=-1)
        h_lat = modulate(x, norm_g[layer, 0], mod_lat[0], mod_lat[1])
        h_ctx = modulate(ctx, norm_g[layer, 0], mod_ctx[0], mod_ctx[1])
        j = layer // 2
        if layer % 2 == 0:
            y_lat, y_ctx = hybrid_attention(h_lat, h_ctx, attn_w_in[j], attn_w_out[j], attn_qk_gain[j],
                                            na_rpb[j], not last)
        else:
            y_lat, y_ctx = bidirectional_mlstm(h_lat, h_ctx, mlstm_w_in[j], mlstm_gate_bias[j],
                                               mlstm_head_gain[j], mlstm_w_out[j], not last)
        x = x + mod_lat[2] * y_lat
        h_lat = modulate(x, norm_g[layer, 1], mod_lat[3], mod_lat[4]).reshape(B * S, D)
        moe_args = (router_w[layer], router_b[layer], expert_w1[layer], expert_b1[layer],
                    expert_w2[layer], expert_b2[layer])
        if last:
            x = x + mod_lat[5] * moe_ffn(h_lat, *moe_args).reshape(B, S, D)
        else:
            ctx = ctx + mod_ctx[2] * y_ctx
            h_ctx = modulate(ctx, norm_g[layer, 1], mod_ctx[3], mod_ctx[4]).reshape(B * Lc, D)
            y = moe_ffn(jnp.concatenate([h_lat, h_ctx], axis=0), *moe_args)
            x = x + mod_lat[5] * y[:B * S].reshape(B, S, D)
            ctx = ctx + mod_ctx[5] * y[B * S:].reshape(B, Lc, D)
    return x
```

```python
import functools

import jax
import jax.numpy as jnp
import numpy as np
from jax import lax
from jax.experimental import pallas as pl
from jax.experimental.pallas import tpu as pltpu

F32 = jnp.float32
BF16 = jnp.bfloat16

D = 2048
B = 4
S = 2048
LC = 256
GRID_W = 64
ROWS = S // GRID_W
HD = 128
NA_H = 8
GQA_H = 8
GQA_KV = 2
NA_KH = 8
NA_KW = 16
ROPE_THETA = 10000.0
ML_H = 8
ML_DK = 128
ML_DV = 256
GATE_SOFTCAP = 15.0
N_EXP = 32
TOP_K = 4
D_EXP = D
SWIGLU_ALPHA = 1.702
SWIGLU_LIMIT = 7.0
NORM_EPS = 1e-6
ATTN_IN = 3 * NA_H * HD + GQA_H * HD + 2 * GQA_KV * HD
ML_MAIN = 2 * ML_H * ML_DK + 2 * ML_H * ML_DV

T_LAT = B * S
T_CTX = B * LC
T_ALL = T_LAT + T_CTX
MOD_CTX_ROW = B

LANES = 128

TM = 1024
TN = 512
TNORM = 512
ML_CHUNK = 256
MOE_BM = 512
MOE_FH = 512
NEG = -1e30


def _mod_row(i, tm):
    n_lat = T_LAT // tm
    per_b = S // tm
    return jnp.where(i < n_lat, i // per_b, MOD_CTX_ROW)


def _mm_kernel(*refs, silu, has_bias, has_res):
    a_ref, w_ref = refs[0], refs[1]
    pos = 2
    bias_ref = res_ref = gate_ref = None
    if has_bias:
        bias_ref = refs[pos]
        pos += 1
    if has_res:
        res_ref, gate_ref = refs[pos], refs[pos + 1]
        pos += 2
    o_ref = refs[pos]
    a = a_ref[...]
    if silu:
        a = a.astype(F32)
        a = a * (1.0 / (1.0 + jnp.exp(-a)))
    acc = jnp.dot(a.astype(BF16), w_ref[...].astype(BF16), preferred_element_type=F32)
    if has_bias:
        acc = acc + bias_ref[...]
    if has_res:
        acc = res_ref[...] + gate_ref[0] * acc
    o_ref[...] = acc.astype(o_ref.dtype)


def _matmul(a, w, n_out, *, out_dtype, tm=TM, tn=TN, m_rows=None, w_row_block=0, silu=False, bias=None,
            res=None, mod=None, gate_chunk=None, name="mm"):
    m_rows = a.shape[0] if m_rows is None else m_rows
    k = a.shape[1]
    grid = (m_rows // tm, n_out // tn)
    in_specs = [pl.BlockSpec((tm, k), lambda i, j: (i, 0)),
                pl.BlockSpec((k, tn), lambda i, j: (w_row_block, j))]
    args = [a, w]
    if bias is not None:
        in_specs.append(pl.BlockSpec((1, tn), lambda i, j: (0, j)))
        args.append(bias)
    if res is not None:
        per = D // tn
        in_specs.append(pl.BlockSpec((tm, tn), lambda i, j: (i, j)))
        in_specs.append(pl.BlockSpec((1, 1, tn), lambda i, j: (_mod_row(i, tm), 0, gate_chunk * per + j)))
        args += [res, mod]
    return pl.pallas_call(
        functools.partial(_mm_kernel, silu=silu, has_bias=bias is not None, has_res=res is not None),
        out_shape=jax.ShapeDtypeStruct((m_rows, n_out), out_dtype),
        grid=grid, in_specs=in_specs,
        out_specs=pl.BlockSpec((tm, tn), lambda i, j: (i, j)),
        compiler_params=pltpu.CompilerParams(dimension_semantics=("parallel", "parallel")),
        name=name,
    )(*args)


def _norm_kernel(*refs, router):
    x_ref, g_ref, shift_ref, scale_ref = refs[:4]
    x = x_ref[...]
    ms = jnp.mean(x * x, axis=-1, keepdims=True)
    y = x * lax.rsqrt(ms + NORM_EPS) * g_ref[...]
    h = y * (1.0 + scale_ref[0]) + shift_ref[0]
    hb = h.astype(BF16)
    if not router:
        refs[4][...] = hb
        return
    rw_ref, rb_ref, h_ref, e_ref, p_ref = refs[4:]
    h_ref[...] = hb
    logits = jnp.dot(hb, rw_ref[...].astype(BF16), preferred_element_type=F32) + rb_ref[...]
    lane = lax.broadcasted_iota(jnp.int32, logits.shape, 1)
    e_out = jnp.zeros(logits.shape, jnp.int32)
    v_out = jnp.full(logits.shape, NEG, F32)
    work = logits
    for kk in range(TOP_K):
        mx = jnp.max(work, axis=-1, keepdims=True)
        idx = jnp.min(jnp.where(work == mx, lane, LANES), axis=-1, keepdims=True)
        e_out = jnp.where(lane == kk, idx, e_out)
        v_out = jnp.where(lane == kk, mx, v_out)
        work = jnp.where(lane == idx, -jnp.inf, work)
    top0 = jnp.max(v_out, axis=-1, keepdims=True)
    pe = jnp.exp(v_out - top0)
    e_ref[...] = e_out
    p_ref[...] = pe / jnp.sum(pe, axis=-1, keepdims=True)


def _norm_mod(xs, g, mod, shift_chunk, scale_chunk, *, n_rows, router_w=None, router_b=None, name="norm"):
    tm = TNORM
    in_specs = [pl.BlockSpec((tm, D), lambda i: (i, 0)),
                pl.BlockSpec((1, D), lambda i: (0, 0)),
                pl.BlockSpec((1, 1, D), lambda i: (_mod_row(i, tm), 0, shift_chunk)),
                pl.BlockSpec((1, 1, D), lambda i: (_mod_row(i, tm), 0, scale_chunk))]
    args = [xs, g, mod, mod]
    out_shape = [jax.ShapeDtypeStruct((n_rows, D), BF16)]
    out_specs = [pl.BlockSpec((tm, D), lambda i: (i, 0))]
    router = router_w is not None
    if router:
        in_specs += [pl.BlockSpec((D, LANES), lambda i: (0, 0)),
                     pl.BlockSpec((1, LANES), lambda i: (0, 0))]
        args += [router_w, router_b]
        out_shape += [jax.ShapeDtypeStruct((n_rows, LANES), jnp.int32),
                      jax.ShapeDtypeStruct((n_rows, LANES), F32)]
        out_specs += [pl.BlockSpec((tm, LANES), lambda i: (i, 0))] * 2
    out = pl.pallas_call(
        functools.partial(_norm_kernel, router=router),
        out_shape=out_shape, grid=(n_rows // tm,), in_specs=in_specs, out_specs=out_specs,
        compiler_params=pltpu.CompilerParams(dimension_semantics=("parallel",)),
        name=name,
    )(*args)
    return out if router else out[0]


N_QK_HEADS = 2 * NA_H + GQA_H + GQA_KV


def _qk_col(j):
    return jnp.where(j < 2 * NA_H, j, j + NA_H)


def _qk_gain_row(j):
    return jnp.where(j < NA_H, 0, jnp.where(j < 2 * NA_H, 1, jnp.where(j < 2 * NA_H + GQA_H, 2, 3)))


def _qk_kernel(x_ref, gain_ref, cos_ref, sin_ref, o_ref, *, tm):
    i = pl.program_id(0)
    j = pl.program_id(1)
    y = x_ref[...].astype(F32)
    ms = jnp.mean(y * y, axis=-1, keepdims=True)
    yn = y * lax.rsqrt(ms + NORM_EPS) * gain_ref[0]
    lane = lax.broadcasted_iota(jnp.int32, yn.shape, 1)
    nxt = pltpu.roll(yn, LANES - 1, 1)
    prv = pltpu.roll(yn, 1, 1)
    partner = jnp.where(lane % 2 == 0, nxt, prv)
    yr = yn * cos_ref[...] + partner * sin_ref[...]
    do_rope = jnp.logical_and(j >= 2 * NA_H, i < T_LAT // tm)
    o_ref[...] = jnp.where(do_rope, yr, yn).astype(o_ref.dtype)


def _qk_prep(qkv, gain, cos_rep, sin_signed):
    tm = TNORM
    per_b = S // tm
    return pl.pallas_call(
        functools.partial(_qk_kernel, tm=tm),
        out_shape=jax.ShapeDtypeStruct(qkv.shape, qkv.dtype),
        grid=(T_ALL // tm, N_QK_HEADS),
        in_specs=[pl.BlockSpec((tm, HD), lambda i, j: (i, _qk_col(j))),
                  pl.BlockSpec((1, 1, HD), lambda i, j: (_qk_gain_row(j), 0, 0)),
                  pl.BlockSpec((tm, HD), lambda i, j: (i % per_b, 0)),
                  pl.BlockSpec((tm, HD), lambda i, j: (i % per_b, 0))],
        out_specs=pl.BlockSpec((tm, HD), lambda i, j: (i, _qk_col(j))),
        input_output_aliases={0: 0},
        compiler_params=pltpu.CompilerParams(dimension_semantics=("parallel", "parallel")),
        name="qk_prep",
    )(qkv, gain, cos_rep, sin_signed)


def _rope_tables():
    t = np.arange(S)
    row = (t // GRID_W).astype(np.float32)
    col = (t % GRID_W).astype(np.float32)
    axis_dim = HD // 2
    inv_freq = jnp.asarray(ROPE_THETA, F32) ** (-jnp.arange(0, axis_dim, 2, dtype=F32) / axis_dim)
    ang = jnp.concatenate([jnp.asarray(row)[:, None] * inv_freq, jnp.asarray(col)[:, None] * inv_freq], axis=-1)
    cos, sin = jnp.cos(ang), jnp.sin(ang)
    cos_rep = jnp.repeat(cos, 2, axis=-1)
    sin_signed = jnp.stack([-sin, sin], axis=-1).reshape(S, HD)
    return cos_rep, sin_signed


COL_NA_Q, COL_NA_K, COL_NA_V = 0, NA_H, 2 * NA_H
COL_G_Q, COL_G_K, COL_G_V = 3 * NA_H, 3 * NA_H + GQA_H, 3 * NA_H + GQA_H + GQA_KV
ATT_SCALE = HD ** -0.5
_NT = (((1,), (1,)), ((), ()))


def _attn_kernel(*refs, two):
    if two:
        q_ref, k1_ref, v1_ref, k2_ref, v2_ref, _, o_ref = refs
    else:
        q_ref, k1_ref, v1_ref, _, o_ref = refs
    q = q_ref[...]
    s1 = lax.dot_general(q, k1_ref[...], _NT, preferred_element_type=F32) * ATT_SCALE
    m = jnp.max(s1, axis=-1, keepdims=True)
    if two:
        s2 = lax.dot_general(q, k2_ref[...], _NT, preferred_element_type=F32) * ATT_SCALE
        m = jnp.maximum(m, jnp.max(s2, axis=-1, keepdims=True))
    p1 = jnp.exp(s1 - m)
    l = jnp.sum(p1, axis=-1, keepdims=True)
    o = jnp.dot(p1.astype(BF16), v1_ref[...], preferred_element_type=F32)
    if two:
        p2 = jnp.exp(s2 - m)
        l = l + jnp.sum(p2, axis=-1, keepdims=True)
        o = o + jnp.dot(p2.astype(BF16), v2_ref[...], preferred_element_type=F32)
    o_ref[...] = (o / l).astype(o_ref.dtype)


def _gqa_lat(qkv, o_buf):
    tq = 512
    nq = S // tq
    ctx_blk = T_LAT // LC
    grp = GQA_H // GQA_KV
    return pl.pallas_call(
        functools.partial(_attn_kernel, two=True),
        out_shape=jax.ShapeDtypeStruct(o_buf.shape, o_buf.dtype),
        grid=(B, GQA_H, nq),
        in_specs=[pl.BlockSpec((tq, HD), lambda b, h, t: (b * nq + t, COL_G_Q + h)),
                  pl.BlockSpec((LC, HD), lambda b, h, t: (ctx_blk + b, COL_G_K + h // grp)),
                  pl.BlockSpec((LC, HD), lambda b, h, t: (ctx_blk + b, COL_G_V + h // grp)),
                  pl.BlockSpec((S, HD), lambda b, h, t: (b, COL_G_K + h // grp)),
                  pl.BlockSpec((S, HD), lambda b, h, t: (b, COL_G_V + h // grp)),
                  pl.BlockSpec(memory_space=pl.ANY)],
        out_specs=pl.BlockSpec((tq, HD), lambda b, h, t: (b * nq + t, NA_H + h)),
        input_output_aliases={5: 0},
        compiler_params=pltpu.CompilerParams(dimension_semantics=("parallel", "parallel", "parallel")),
        name="gqa_lat",
    )(qkv, qkv, qkv, qkv, qkv, o_buf)


def _ctx_attn(qkv, o_buf):
    ctx_blk = T_LAT // LC
    grp = GQA_H // GQA_KV

    def qcol(h):
        return jnp.where(h < NA_H, COL_NA_Q + h, COL_G_Q + h - NA_H)

    def kcol(h):
        return jnp.where(h < NA_H, COL_NA_K + h, COL_G_K + (h - NA_H) // grp)

    def vcol(h):
        return jnp.where(h < NA_H, COL_NA_V + h, COL_G_V + (h - NA_H) // grp)

    return pl.pallas_call(
        functools.partial(_attn_kernel, two=False),
        out_shape=jax.ShapeDtypeStruct(o_buf.shape, o_buf.dtype),
        grid=(B, NA_H + GQA_H),
        in_specs=[pl.BlockSpec((LC, HD), lambda b, h: (ctx_blk + b, qcol(h))),
                  pl.BlockSpec((LC, HD), lambda b, h: (ctx_blk + b, kcol(h))),
                  pl.BlockSpec((LC, HD), lambda b, h: (ctx_blk + b, vcol(h))),
                  pl.BlockSpec(memory_space=pl.ANY)],
        out_specs=pl.BlockSpec((LC, HD), lambda b, h: (ctx_blk + b, h)),
        input_output_aliases={3: 0},
        compiler_params=pltpu.CompilerParams(dimension_semantics=("parallel", "parallel")),
        name="ctx_attn",
    )(qkv, qkv, qkv, o_buf)


def _na_kernel(q_ref, k_ref, v_ref, kc_ref, vc_ref, bias_ref, o_ref):
    kc = kc_ref[...]
    vc = vc_ref[...]
    n_win = NA_KH * GRID_W

    def body(r, carry):
        r0 = jnp.clip(r - NA_KH // 2, 0, ROWS - NA_KH)
        rel0 = r0 - r + NA_KH - 1
        q = q_ref[pl.ds(pl.multiple_of(r * GRID_W, GRID_W), GRID_W), :]
        kw = k_ref[pl.ds(pl.multiple_of(r0 * GRID_W, GRID_W), n_win), :]
        vw = v_ref[pl.ds(pl.multiple_of(r0 * GRID_W, GRID_W), n_win), :]
        sw = lax.dot_general(q, kw, _NT, preferred_element_type=F32) * ATT_SCALE + bias_ref[0, rel0]
        sc = lax.dot_general(q, kc, _NT, preferred_element_type=F32) * ATT_SCALE
        m = jnp.maximum(jnp.max(sw, axis=-1, keepdims=True), jnp.max(sc, axis=-1, keepdims=True))
        pw = jnp.exp(sw - m)
        pc = jnp.exp(sc - m)
        l = jnp.sum(pw, axis=-1, keepdims=True) + jnp.sum(pc, axis=-1, keepdims=True)
        o = (jnp.dot(pw.astype(BF16), vw, preferred_element_type=F32)
             + jnp.dot(pc.astype(BF16), vc, preferred_element_type=F32))
        o_ref[pl.ds(pl.multiple_of(r * GRID_W, GRID_W), GRID_W), :] = (o / l).astype(o_ref.dtype)
        return carry

    lax.fori_loop(0, ROWS, body, 0)


def _na_bias_table(rpb):
    rel0 = np.arange(NA_KH)[:, None, None, None]
    qc = np.arange(GRID_W)[None, :, None, None]
    rr = np.arange(NA_KH)[None, None, :, None]
    kc = np.arange(GRID_W)[None, None, None, :]
    start = np.clip(qc - NA_KW // 2, 0, GRID_W - NA_KW)
    valid = np.broadcast_to((kc >= start) & (kc < start + NA_KW), (NA_KH, GRID_W, NA_KH, GRID_W))
    row_rel = np.broadcast_to(rel0 + rr, valid.shape)
    col_rel = np.clip(np.broadcast_to(kc - qc + NA_KW - 1, valid.shape), 0, 2 * NA_KW - 2)
    tab = rpb[:, row_rel, col_rel]
    tab = jnp.where(jnp.asarray(valid)[None], tab, NEG)
    return tab.reshape(NA_H, NA_KH, GRID_W, NA_KH * GRID_W)


def _na_lat(qkv, bias_tab):
    ctx_blk = T_LAT // LC
    return pl.pallas_call(
        _na_kernel,
        out_shape=jax.ShapeDtypeStruct((T_ALL, D), BF16),
        grid=(B, NA_H),
        in_specs=[pl.BlockSpec((S, HD), lambda b, h: (b, COL_NA_Q + h)),
                  pl.BlockSpec((S, HD), lambda b, h: (b, COL_NA_K + h)),
                  pl.BlockSpec((S, HD), lambda b, h: (b, COL_NA_V + h)),
                  pl.BlockSpec((LC, HD), lambda b, h: (ctx_blk + b, COL_NA_K + h)),
                  pl.BlockSpec((LC, HD), lambda b, h: (ctx_blk + b, COL_NA_V + h)),
                  pl.BlockSpec((1, NA_KH, GRID_W, NA_KH * GRID_W), lambda b, h: (h, 0, 0, 0))],
        out_specs=pl.BlockSpec((S, HD), lambda b, h: (b, h)),
        compiler_params=pltpu.CompilerParams(dimension_semantics=("parallel", "parallel")),
        name="na_lat",
    )(qkv, qkv, qkv, qkv, qkv, bias_tab)


def _gate_kernel(a_ref, w_ref, b_ref, o_ref):
    g = jnp.dot(a_ref[...], w_ref[...].astype(BF16), preferred_element_type=F32) + b_ref[...]
    g = GATE_SOFTCAP * jnp.tanh(g / GATE_SOFTCAP)
    lane = lax.broadcasted_iota(jnp.int32, g.shape, 1)
    is_forget = (lane // ML_H) % 2 == 1
    log_sig = jnp.minimum(g, 0.0) - jnp.log(1.0 + jnp.exp(-jnp.abs(g)))
    o_ref[...] = jnp.where(is_forget, log_sig, g)


def _mlstm_gates(h, wg, bg):
    tm = TM
    return pl.pallas_call(
        _gate_kernel,
        out_shape=jax.ShapeDtypeStruct((T_ALL, LANES), F32),
        grid=(T_ALL // tm,),
        in_specs=[pl.BlockSpec((tm, D), lambda i: (i, 0)),
                  pl.BlockSpec((D, LANES), lambda i: (0, 0)),
                  pl.BlockSpec((1, LANES), lambda i: (0, 0))],
        out_specs=pl.BlockSpec((tm, LANES), lambda i: (i, 0)),
        compiler_params=pltpu.CompilerParams(dimension_semantics=("parallel",)),
        name="mlstm_gates",
    )(h, wg, bg)


ML_SCALE = ML_DK ** -0.5
_TN = (((0,), (0,)), ((), ()))


def _mlstm_chunk(q, k, v, ig, fg, c_st, n_st, m_st, rev):
    L = q.shape[0]
    ti = lax.broadcasted_iota(jnp.int32, (L, L), 0)
    si = lax.broadcasted_iota(jnp.int32, (L, L), 1)
    eye = ti == si
    before_col = (si >= ti) if rev else (si <= ti)
    before_row = (ti >= si) if rev else (ti <= si)
    f_col = jnp.sum(jnp.where(eye, fg, 0.0), axis=1, keepdims=True)
    i_col = jnp.sum(jnp.where(eye, ig, 0.0), axis=1, keepdims=True)
    b_col = jnp.sum(jnp.where(before_col, fg, 0.0), axis=1, keepdims=True)
    b_row = jnp.sum(jnp.where(before_row, f_col, 0.0), axis=0, keepdims=True)
    total = jnp.sum(fg, axis=1, keepdims=True)
    g_col = total - b_col + i_col
    m_new = jnp.maximum(total + m_st, jnp.max(g_col, axis=0, keepdims=True))
    decay = jnp.exp(total + m_st - m_new)
    wk = jnp.exp(g_col - m_new)
    kw = k.astype(F32) * wk
    c_new = decay * c_st + lax.dot_general(kw.astype(BF16), v, _TN, preferred_element_type=F32)
    n_new = decay * n_st + jnp.sum(kw, axis=0, keepdims=True)

    dmat = jnp.where(before_col, b_col - b_row + ig, NEG)
    inter = b_col + m_st
    m_t = jnp.maximum(inter, jnp.max(dmat, axis=1, keepdims=True))
    a = jnp.exp(inter - m_t)
    qk = lax.dot_general(q, k, _NT, preferred_element_type=F32) * ML_SCALE
    smat = qk * jnp.exp(dmat - m_t)
    num = (a * (jnp.dot(q, c_st.astype(BF16), preferred_element_type=F32) * ML_SCALE)
           + jnp.dot(smat.astype(BF16), v, preferred_element_type=F32))
    den = (a * (jnp.sum(q.astype(F32) * n_st, axis=1, keepdims=True) * ML_SCALE)
           + jnp.sum(smat, axis=1, keepdims=True))
    h = num / jnp.maximum(jnp.abs(den), jnp.exp(-m_t))
    return h, c_new, n_new, m_new


def _mlstm_kernel(qf, kf, vf, gf, qb, kb, vb, gb, hf_ref, hb_ref, cf, nf, mf, cb, nb, mb):
    @pl.when(pl.program_id(1) == 0)
    def _():
        for r in (cf, nf, mf, cb, nb, mb):
            r[...] = jnp.zeros(r.shape, r.dtype)

    gfv = gf[0]
    h, c_new, n_new, m_new = _mlstm_chunk(qf[...], kf[...], vf[...], gfv[0:1], gfv[1:2],
                                          cf[...], nf[...], mf[...], False)
    hf_ref[...] = h
    cf[...] = c_new
    nf[...] = n_new
    mf[...] = m_new
    gbv = gb[0]
    h, c_new, n_new, m_new = _mlstm_chunk(qb[...], kb[...], vb[...], gbv[2:3], gbv[3:4],
                                          cb[...], nb[...], mb[...], True)
    hb_ref[...] = h
    cb[...] = c_new
    nb[...] = n_new
    mb[...] = m_new


def _mlstm_scan(proj, gates):
    L = ML_CHUNK
    n_lat = S // L
    assert LC == L
    ctx_blk = T_LAT // L
    qcol, kcol, vcol = 0, ML_H, (2 * ML_H * ML_DK) // ML_DV

    def fwd(bh, c):
        b = bh // ML_H
        return jnp.where(c == 0, ctx_blk + b, b * n_lat + c - 1)

    def bwd(bh, c):
        b = bh // ML_H
        return jnp.where(c == 0, ctx_blk + b, b * n_lat + n_lat - c)

    def specs(blk):
        return [pl.BlockSpec((L, ML_DK), lambda bh, c: (blk(bh, c), qcol + bh % ML_H)),
                pl.BlockSpec((L, ML_DK), lambda bh, c: (blk(bh, c), kcol + bh % ML_H)),
                pl.BlockSpec((L, ML_DV), lambda bh, c: (blk(bh, c), vcol + bh % ML_H)),
                pl.BlockSpec((1, 4, L), lambda bh, c: (bh % ML_H, 0, blk(bh, c)))]

    out_sds = jax.ShapeDtypeStruct((T_ALL, ML_H * ML_DV), F32)
    return pl.pallas_call(
        _mlstm_kernel,
        out_shape=[out_sds, out_sds],
        grid=(B * ML_H, 1 + n_lat),
        in_specs=specs(fwd) + specs(bwd),
        out_specs=[pl.BlockSpec((L, ML_DV), lambda bh, c: (fwd(bh, c), bh % ML_H)),
                   pl.BlockSpec((L, ML_DV), lambda bh, c: (bwd(bh, c), bh % ML_H))],
        scratch_shapes=[pltpu.VMEM((ML_DK, ML_DV), F32), pltpu.VMEM((1, ML_DK), F32), pltpu.VMEM((1, 1), F32),
                        pltpu.VMEM((ML_DK, ML_DV), F32), pltpu.VMEM((1, ML_DK), F32), pltpu.VMEM((1, 1), F32)],
        compiler_params=pltpu.CompilerParams(dimension_semantics=("parallel", "arbitrary")),
        name="mlstm_scan",
    )(proj, proj, proj, gates, proj, proj, proj, gates)


def _readout_kernel(hf_ref, hb_ref, o_ref, gain_ref, out_ref):
    hs = hf_ref[...] + hb_ref[...]
    o = o_ref[...].astype(F32)
    for hh in range(ML_H):
        sl = slice(hh * ML_DV, (hh + 1) * ML_DV)
        x = hs[:, sl]
        ms = jnp.mean(x * x, axis=-1, keepdims=True)
        y = x * lax.rsqrt(ms + NORM_EPS) * gain_ref[:, sl]
        og = o[:, sl]
        out_ref[:, sl] = (y * (1.0 / (1.0 + jnp.exp(-og)))).astype(out_ref.dtype)


def _mlstm_readout(hf, hb, proj, gain, n_rows):
    tm = TNORM
    wide = ML_H * ML_DV
    ocol = (2 * ML_H * ML_DK + ML_H * ML_DV) // wide
    return pl.pallas_call(
        _readout_kernel,
        out_shape=jax.ShapeDtypeStruct((n_rows, wide), BF16),
        grid=(n_rows // tm,),
        in_specs=[pl.BlockSpec((tm, wide), lambda i: (i, 0)),
                  pl.BlockSpec((tm, wide), lambda i: (i, 0)),
                  pl.BlockSpec((tm, wide), lambda i: (i, ocol)),
                  pl.BlockSpec((1, wide), lambda i: (0, 0))],
        out_specs=pl.BlockSpec((tm, wide), lambda i: (i, 0)),
        compiler_params=pltpu.CompilerParams(dimension_semantics=("parallel",)),
        name="mlstm_readout",
    )(hf, hb, proj, gain)


def _moe_kernel(be_ref, nu_ref, x_ref, w1_ref, b1_ref, w2_ref, b2_ref, sel_ref, y_ref, acc_ref):
    i = pl.program_id(0)
    f = pl.program_id(1)
    nf = pl.num_programs(1)

    @pl.when(i < nu_ref[0])
    def _():
        @pl.when(f == 0)
        def _():
            acc_ref[...] = jnp.zeros(acc_ref.shape, F32)

        h = jnp.dot(x_ref[...], w1_ref[0].astype(BF16), preferred_element_type=F32) + b1_ref[0]
        glu = jnp.minimum(h, SWIGLU_LIMIT)
        glu = glu * (1.0 / (1.0 + jnp.exp(-SWIGLU_ALPHA * glu)))
        lin = jnp.clip(h, -SWIGLU_LIMIT, SWIGLU_LIMIT) + 1.0
        prod = glu * pltpu.roll(lin, h.shape[1] - 1, 1)
        act = jnp.dot(prod.astype(BF16), sel_ref[...], preferred_element_type=F32)
        acc_ref[...] += jnp.dot(act.astype(BF16), w2_ref[0].astype(BF16), preferred_element_type=F32)

        @pl.when(f == nf - 1)
        def _():
            y_ref[...] = (acc_ref[...] + b2_ref[0]).astype(y_ref.dtype)


def _moe_blocks_max(n_tok):
    m = n_tok * TOP_K
    return -(-(m + N_EXP * (MOE_BM - 1)) // MOE_BM)


def _moe_experts(rows, block_expert, n_used, layer, w1, b1, w2, b2, sel):
    bm, fh = MOE_BM, MOE_FH
    n_blocks = rows.shape[0] // bm
    nf = D_EXP // fh
    e0 = layer * N_EXP

    def blk(i, nu):
        return jnp.minimum(i, nu[0] - 1)

    grid_spec = pltpu.PrefetchScalarGridSpec(
        num_scalar_prefetch=2,
        grid=(n_blocks, nf),
        in_specs=[pl.BlockSpec((bm, D), lambda i, f, be, nu: (blk(i, nu), 0)),
                  pl.BlockSpec((1, D, 2 * fh), lambda i, f, be, nu: (e0 + be[blk(i, nu)], 0, f)),
                  pl.BlockSpec((1, 1, 2 * fh), lambda i, f, be, nu: (be[blk(i, nu)], 0, f)),
                  pl.BlockSpec((1, fh, D), lambda i, f, be, nu: (e0 + be[blk(i, nu)], f, 0)),
                  pl.BlockSpec((1, 1, D), lambda i, f, be, nu: (be[blk(i, nu)], 0, 0)),
                  pl.BlockSpec((2 * fh, fh), lambda i, f, be, nu: (0, 0))],
        out_specs=pl.BlockSpec((bm, D), lambda i, f, be, nu: (blk(i, nu), 0)),
        scratch_shapes=[pltpu.VMEM((bm, D), F32)],
    )
    return pl.pallas_call(
        _moe_kernel,
        out_shape=jax.ShapeDtypeStruct((n_blocks * bm, D), BF16),
        grid_spec=grid_spec,
        compiler_params=pltpu.CompilerParams(dimension_semantics=("arbitrary", "arbitrary")),
        name="moe_experts",
    )(block_expert, n_used, rows, w1, b1, w2, b2, sel)


def _combine_kernel(x_ref, y_ref, p_ref, gate_ref, o_ref):
    p = p_ref[...]
    acc = p[:, 0:1] * y_ref[:, 0:D].astype(F32)
    for kk in range(1, TOP_K):
        acc = acc + p[:, kk:kk + 1] * y_ref[:, kk * D:(kk + 1) * D].astype(F32)
    o_ref[...] = x_ref[...] + gate_ref[0] * acc


def _moe_combine(xs, yg, probs, mod, gate_chunk, n_rows):
    tm = TNORM
    return pl.pallas_call(
        _combine_kernel,
        out_shape=jax.ShapeDtypeStruct((n_rows, D), F32),
        grid=(n_rows // tm,),
        in_specs=[pl.BlockSpec((tm, D), lambda i: (i, 0)),
                  pl.BlockSpec((tm, TOP_K * D), lambda i: (i, 0)),
                  pl.BlockSpec((tm, LANES), lambda i: (i, 0)),
                  pl.BlockSpec((1, 1, D), lambda i: (_mod_row(i, tm), 0, gate_chunk))],
        out_specs=pl.BlockSpec((tm, D), lambda i: (i, 0)),
        compiler_params=pltpu.CompilerParams(dimension_semantics=("parallel",)),
        name="moe_combine",
    )(xs, yg, probs, mod)


def _moe_route(top_e, n_tok):
    m = n_tok * TOP_K
    bm = MOE_BM
    n_blocks = _moe_blocks_max(n_tok)
    e_flat = top_e.reshape(m)
    onehot = (e_flat[:, None] == jnp.arange(N_EXP, dtype=jnp.int32)[None, :]).astype(jnp.int32)
    csum = jnp.cumsum(onehot, axis=0)
    counts = csum[-1]
    rank = jnp.take_along_axis(csum, e_flat[:, None], axis=1)[:, 0] - 1
    padded = (counts + bm - 1) // bm * bm
    pad_end = jnp.cumsum(padded)
    pad_start = pad_end - padded
    dest = pad_start[e_flat] + rank
    row_tok = jnp.zeros((n_blocks * bm,), jnp.int32).at[dest].set(jnp.arange(m, dtype=jnp.int32) // TOP_K)
    n_used = (pad_end[-1] // bm).astype(jnp.int32).reshape(1)
    block_expert = jnp.minimum(
        jnp.searchsorted(pad_end, jnp.arange(n_blocks, dtype=jnp.int32) * bm, side="right"),
        N_EXP - 1).astype(jnp.int32)
    return dest, row_tok, block_expert, n_used


def _moe_layer(xs, n_rows, g, mod, router_w, router_b, layer, w1_all, b1, w2_all, b2, sel, name):
    rw = jnp.pad(router_w, ((0, 0), (0, LANES - N_EXP)))
    rb = jnp.pad(router_b, (0, LANES - N_EXP), constant_values=NEG).reshape(1, LANES)
    h, top_e, probs = _norm_mod(xs, g, mod, 3, 4, n_rows=n_rows, router_w=rw, router_b=rb, name=name + "_norm")
    dest, row_tok, block_expert, n_used = _moe_route(top_e[:, :TOP_K], n_rows)
    rows = jnp.take(h, row_tok, axis=0)
    y_rows = _moe_experts(rows, block_expert, n_used, layer, w1_all, b1.reshape(N_EXP, 1, 2 * D_EXP), w2_all,
                          b2.reshape(N_EXP, 1, D), sel)
    yg = jnp.take(y_rows, dest, axis=0).reshape(n_rows, TOP_K * D)
    return _moe_combine(xs, yg, probs, mod, 5, n_rows)


def _select_matrix():
    r = np.arange(2 * MOE_FH)[:, None]
    c = np.arange(MOE_FH)[None, :]
    return jnp.asarray(r == 2 * c, BF16)


def kernel(x, c, ctx, c_ctx, ada_w, ada_b, norm_g, attn_w_in, attn_w_out, attn_qk_gain, na_rpb, mlstm_w_in,
           mlstm_gate_bias, mlstm_head_gain, mlstm_w_out, router_w, router_b, expert_w1, expert_b1,
           expert_w2, expert_b2):
    xs = jnp.concatenate([x.reshape(T_LAT, D), ctx.reshape(T_CTX, D)], axis=0)
    cc = jnp.zeros((8, D), F32).at[:B].set(c).at[MOD_CTX_ROW].set(c_ctx)
    sel = _select_matrix()

    n_layers = ada_w.shape[0]
    ada_w2 = ada_w.reshape(n_layers * D, 6 * D)
    w1_all = expert_w1.reshape(n_layers * N_EXP, D, 2 * D_EXP)
    w2_all = expert_w2.reshape(n_layers * N_EXP, D_EXP, D)

    def ada(layer):
        m = _matmul(cc, ada_w2, 6 * D, out_dtype=F32, tm=8, w_row_block=layer, silu=True,
                    bias=ada_b[layer].reshape(1, 6 * D), name="adaln")
        return m.reshape(8, 1, 6 * D)

    mod = ada(0)
    h = _norm_mod(xs, norm_g[0, 0].reshape(1, D), mod, 0, 1, n_rows=T_ALL, name="l0_norm1")
    qkv = _matmul(h, attn_w_in[0], ATTN_IN, out_dtype=BF16, name="attn_in")
    cos_rep, sin_signed = _rope_tables()
    qkv = _qk_prep(qkv, attn_qk_gain[0].reshape(4, 1, HD), cos_rep, sin_signed)
    o_buf = _na_lat(qkv, _na_bias_table(na_rpb[0]))
    o_buf = _gqa_lat(qkv, o_buf)
    o_buf = _ctx_attn(qkv, o_buf)
    xs = _matmul(o_buf, attn_w_out[0], D, out_dtype=F32, res=xs, mod=mod, gate_chunk=2, name="attn_out")
    xs = _moe_layer(xs, T_ALL, norm_g[0, 1].reshape(1, D), mod, router_w[0], router_b[0],
                    0, w1_all, expert_b1[0], w2_all, expert_b2[0], sel, "l0_moe")

    mod = ada(1)
    h = _norm_mod(xs, norm_g[1, 0].reshape(1, D), mod, 0, 1, n_rows=T_ALL, name="l1_norm1")
    proj = _matmul(h, mlstm_w_in[0], ML_MAIN, out_dtype=BF16, name="mlstm_in")
    wg = jnp.pad(mlstm_w_in[0][:, ML_MAIN:], ((0, 0), (0, LANES - 4 * ML_H)))
    bg = jnp.pad(mlstm_gate_bias[0].reshape(-1), (0, LANES - 4 * ML_H)).reshape(1, LANES)
    g = _mlstm_gates(h, wg, bg)
    gates = g[:, :4 * ML_H].T.reshape(4, ML_H, T_ALL).transpose(1, 0, 2)
    hf, hb = _mlstm_scan(proj, gates)
    hn = _mlstm_readout(hf, hb, proj, mlstm_head_gain[0].reshape(1, ML_H * ML_DV), T_LAT)
    xs = _matmul(hn, mlstm_w_out[0], D, out_dtype=F32, m_rows=T_LAT, res=xs, mod=mod, gate_chunk=2,
                 name="mlstm_out")
    out = _moe_layer(xs, T_LAT, norm_g[1, 1].reshape(1, D), mod, router_w[1], router_b[1],
                     1, w1_all, expert_b1[1], w2_all, expert_b2[1], sel, "l1_moe")
    return out.reshape(B, S, D)
```

```python
import functools

import jax
import jax.numpy as jnp
import numpy as np
from jax import lax
from jax.experimental import pallas as pl
from jax.experimental.pallas import tpu as pltpu

F32 = jnp.float32
BF16 = jnp.bfloat16

D = 2048
B = 4
S = 2048
LC = 256
GRID_W = 64
ROWS = S // GRID_W
HD = 128
NA_H = 8
GQA_H = 8
GQA_KV = 2
NA_KH = 8
NA_KW = 16
ROPE_THETA = 10000.0
ML_H = 8
ML_DK = 128
ML_DV = 256
GATE_SOFTCAP = 15.0
N_EXP = 32
TOP_K = 4
D_EXP = D
SWIGLU_ALPHA = 1.702
SWIGLU_LIMIT = 7.0
NORM_EPS = 1e-6
ATTN_IN = 3 * NA_H * HD + GQA_H * HD + 2 * GQA_KV * HD
ML_MAIN = 2 * ML_H * ML_DK + 2 * ML_H * ML_DV

T_LAT = B * S
T_CTX = B * LC
T_ALL = T_LAT + T_CTX
MOD_CTX_ROW = B

LANES = 128

TM = 1024
TN = 512
TNORM = 512
ML_CHUNK = 256
MOE_BM = 512
MOE_FH = 512
NEG = -1e30


def _mod_row(i, tm):
    n_lat = T_LAT // tm
    per_b = S // tm
    return jnp.where(i < n_lat, i // per_b, MOD_CTX_ROW)


def _mm_kernel(*refs, silu, has_bias, has_res):
    a_ref, w_ref = refs[0], refs[1]
    pos = 2
    bias_ref = res_ref = gate_ref = None
    if has_bias:
        bias_ref = refs[pos]
        pos += 1
    if has_res:
        res_ref, gate_ref = refs[pos], refs[pos + 1]
        pos += 2
    o_ref = refs[pos]
    a = a_ref[...]
    if silu:
        a = a.astype(F32)
        a = a * (1.0 / (1.0 + jnp.exp(-a)))
    acc = jnp.dot(a.astype(BF16), w_ref[...].astype(BF16), preferred_element_type=F32)
    if has_bias:
        acc = acc + bias_ref[...]
    if has_res:
        acc = res_ref[...] + gate_ref[0] * acc
    o_ref[...] = acc.astype(o_ref.dtype)


def _matmul(a, w, n_out, *, out_dtype, tm=TM, tn=TN, m_rows=None, w_row_block=0, silu=False, bias=None,
            res=None, mod=None, gate_chunk=None, name="mm"):
    m_rows = a.shape[0] if m_rows is None else m_rows
    k = a.shape[1]
    grid = (m_rows // tm, n_out // tn)
    in_specs = [pl.BlockSpec((tm, k), lambda i, j: (i, 0)),
                pl.BlockSpec((k, tn), lambda i, j: (w_row_block, j))]
    args = [a, w]
    if bias is not None:
        in_specs.append(pl.BlockSpec((1, tn), lambda i, j: (0, j)))
        args.append(bias)
    if res is not None:
        per = D // tn
        in_specs.append(pl.BlockSpec((tm, tn), lambda i, j: (i, j)))
        in_specs.append(pl.BlockSpec((1, 1, tn), lambda i, j: (_mod_row(i, tm), 0, gate_chunk * per + j)))
        args += [res, mod]
    return pl.pallas_call(
        functools.partial(_mm_kernel, silu=silu, has_bias=bias is not None, has_res=res is not None),
        out_shape=jax.ShapeDtypeStruct((m_rows, n_out), out_dtype),
        grid=grid, in_specs=in_specs,
        out_specs=pl.BlockSpec((tm, tn), lambda i, j: (i, j)),
        compiler_params=pltpu.CompilerParams(dimension_semantics=("parallel", "parallel")),
        name=name,
    )(*args)


def _norm_kernel(*refs, router):
    x_ref, g_ref, shift_ref, scale_ref = refs[:4]
    x = x_ref[...]
    ms = jnp.mean(x * x, axis=-1, keepdims=True)
    y = x * lax.rsqrt(ms + NORM_EPS) * g_ref[...]
    h = y * (1.0 + scale_ref[0]) + shift_ref[0]
    hb = h.astype(BF16)
    if not router:
        refs[4][...] = hb
        return
    rw_ref, rb_ref, h_ref, e_ref, p_ref = refs[4:]
    h_ref[...] = h
    logits = jnp.dot(hb, rw_ref[...].astype(BF16), preferred_element_type=F32) + rb_ref[...]
    lane = lax.broadcasted_iota(jnp.int32, logits.shape, 1)
    e_out = jnp.zeros(logits.shape, jnp.int32)
    v_out = jnp.full(logits.shape, NEG, F32)
    work = logits
    for kk in range(TOP_K):
        mx = jnp.max(work, axis=-1, keepdims=True)
        idx = jnp.min(jnp.where(work == mx, lane, LANES), axis=-1, keepdims=True)
        e_out = jnp.where(lane == kk, idx, e_out)
        v_out = jnp.where(lane == kk, mx, v_out)
        work = jnp.where(lane == idx, -jnp.inf, work)
    top0 = jnp.max(v_out, axis=-1, keepdims=True)
    pe = jnp.exp(v_out - top0)
    e_ref[...] = e_out
    p_ref[...] = pe / jnp.sum(pe, axis=-1, keepdims=True)


def _norm_mod(xs, g, mod, shift_chunk, scale_chunk, *, n_rows, router_w=None, router_b=None, name="norm"):
    tm = TNORM
    in_specs = [pl.BlockSpec((tm, D), lambda i: (i, 0)),
                pl.BlockSpec((1, D), lambda i: (0, 0)),
                pl.BlockSpec((1, 1, D), lambda i: (_mod_row(i, tm), 0, shift_chunk)),
                pl.BlockSpec((1, 1, D), lambda i: (_mod_row(i, tm), 0, scale_chunk))]
    args = [xs, g, mod, mod]
    router = router_w is not None
    out_shape = [jax.ShapeDtypeStruct((n_rows, D), F32 if router else BF16)]
    out_specs = [pl.BlockSpec((tm, D), lambda i: (i, 0))]
    if router:
        in_specs += [pl.BlockSpec((D, LANES), lambda i: (0, 0)),
                     pl.BlockSpec((1, LANES), lambda i: (0, 0))]
        args += [router_w, router_b]
        out_shape += [jax.ShapeDtypeStruct((n_rows, LANES), jnp.int32),
                      jax.ShapeDtypeStruct((n_rows, LANES), F32)]
        out_specs += [pl.BlockSpec((tm, LANES), lambda i: (i, 0))] * 2
    out = pl.pallas_call(
        functools.partial(_norm_kernel, router=router),
        out_shape=out_shape, grid=(n_rows // tm,), in_specs=in_specs, out_specs=out_specs,
        compiler_params=pltpu.CompilerParams(dimension_semantics=("parallel",)),
        name=name,
    )(*args)
    return out if router else out[0]


QK_PAIR = 2
N_QK_SLOTS = (2 * NA_H + GQA_H + GQA_KV) // QK_PAIR
QK_NA_SLOTS = 2 * NA_H // QK_PAIR


def _qk_col(j):
    return jnp.where(j < QK_NA_SLOTS, j, j + NA_H // QK_PAIR)


def _qk_gain_row(j):
    half = NA_H // QK_PAIR
    return jnp.where(j < half, 0, jnp.where(j < 2 * half, 1, jnp.where(j < 3 * half, 2, 3)))


def _qk_kernel(x_ref, gain_ref, cos_ref, sin_ref, o_ref, *, tm):
    i = pl.program_id(0)
    j = pl.program_id(1)
    do_rope = jnp.logical_and(j >= QK_NA_SLOTS, i < T_LAT // tm)
    for hh in range(QK_PAIR):
        sl = slice(hh * HD, (hh + 1) * HD)
        y = x_ref[:, sl].astype(F32)
        ms = jnp.mean(y * y, axis=-1, keepdims=True)
        yn = y * lax.rsqrt(ms + NORM_EPS) * gain_ref[0]
        lane = lax.broadcasted_iota(jnp.int32, yn.shape, 1)
        nxt = pltpu.roll(yn, LANES - 1, 1)
        prv = pltpu.roll(yn, 1, 1)
        partner = jnp.where(lane % 2 == 0, nxt, prv)
        yr = yn * cos_ref[...] + partner * sin_ref[...]
        o_ref[:, sl] = jnp.where(do_rope, yr, yn).astype(o_ref.dtype)


def _qk_prep(qkv, gain, cos_rep, sin_signed):
    tm = TM
    per_b = S // tm
    wide = QK_PAIR * HD
    return pl.pallas_call(
        functools.partial(_qk_kernel, tm=tm),
        out_shape=jax.ShapeDtypeStruct(qkv.shape, qkv.dtype),
        grid=(T_ALL // tm, N_QK_SLOTS),
        in_specs=[pl.BlockSpec((tm, wide), lambda i, j: (i, _qk_col(j))),
                  pl.BlockSpec((1, 1, HD), lambda i, j: (_qk_gain_row(j), 0, 0)),
                  pl.BlockSpec((tm, HD), lambda i, j: (i % per_b, 0)),
                  pl.BlockSpec((tm, HD), lambda i, j: (i % per_b, 0))],
        out_specs=pl.BlockSpec((tm, wide), lambda i, j: (i, _qk_col(j))),
        input_output_aliases={0: 0},
        compiler_params=pltpu.CompilerParams(dimension_semantics=("parallel", "parallel")),
        name="qk_prep",
    )(qkv, gain, cos_rep, sin_signed)


def _rope_tables():
    t = np.arange(S)
    row = (t // GRID_W).astype(np.float32)
    col = (t % GRID_W).astype(np.float32)
    axis_dim = HD // 2
    inv_freq = jnp.asarray(ROPE_THETA, F32) ** (-jnp.arange(0, axis_dim, 2, dtype=F32) / axis_dim)
    ang = jnp.concatenate([jnp.asarray(row)[:, None] * inv_freq, jnp.asarray(col)[:, None] * inv_freq], axis=-1)
    cos, sin = jnp.cos(ang), jnp.sin(ang)
    cos_rep = jnp.repeat(cos, 2, axis=-1)
    sin_signed = jnp.stack([-sin, sin], axis=-1).reshape(S, HD)
    return cos_rep, sin_signed


COL_NA_Q, COL_NA_K, COL_NA_V = 0, NA_H, 2 * NA_H
COL_G_Q, COL_G_K, COL_G_V = 3 * NA_H, 3 * NA_H + GQA_H, 3 * NA_H + GQA_H + GQA_KV
ATT_SCALE = HD ** -0.5
_NT = (((1,), (1,)), ((), ()))


def _attn_kernel(*refs, two):
    if two:
        q_ref, k1_ref, v1_ref, k2_ref, v2_ref, _, o_ref = refs
    else:
        q_ref, k1_ref, v1_ref, _, o_ref = refs
    q = q_ref[...]
    s1 = lax.dot_general(q, k1_ref[...], _NT, preferred_element_type=F32) * ATT_SCALE
    m = jnp.max(s1, axis=-1, keepdims=True)
    if two:
        s2 = lax.dot_general(q, k2_ref[...], _NT, preferred_element_type=F32) * ATT_SCALE
        m = jnp.maximum(m, jnp.max(s2, axis=-1, keepdims=True))
    p1 = jnp.exp(s1 - m)
    l = jnp.sum(p1, axis=-1, keepdims=True)
    o = jnp.dot(p1.astype(BF16), v1_ref[...], preferred_element_type=F32)
    if two:
        p2 = jnp.exp(s2 - m)
        l = l + jnp.sum(p2, axis=-1, keepdims=True)
        o = o + jnp.dot(p2.astype(BF16), v2_ref[...], preferred_element_type=F32)
    o_ref[...] = (o / l).astype(o_ref.dtype)


def _gqa_lat(qkv, o_buf):
    tq = 512
    nq = S // tq
    ctx_blk = T_LAT // LC
    grp = GQA_H // GQA_KV
    return pl.pallas_call(
        functools.partial(_attn_kernel, two=True),
        out_shape=jax.ShapeDtypeStruct(o_buf.shape, o_buf.dtype),
        grid=(B, GQA_H, nq),
        in_specs=[pl.BlockSpec((tq, HD), lambda b, h, t: (b * nq + t, COL_G_Q + h)),
                  pl.BlockSpec((LC, HD), lambda b, h, t: (ctx_blk + b, COL_G_K + h // grp)),
                  pl.BlockSpec((LC, HD), lambda b, h, t: (ctx_blk + b, COL_G_V + h // grp)),
                  pl.BlockSpec((S, HD), lambda b, h, t: (b, COL_G_K + h // grp)),
                  pl.BlockSpec((S, HD), lambda b, h, t: (b, COL_G_V + h // grp)),
                  pl.BlockSpec(memory_space=pl.ANY)],
        out_specs=pl.BlockSpec((tq, HD), lambda b, h, t: (b * nq + t, NA_H + h)),
        input_output_aliases={5: 0},
        compiler_params=pltpu.CompilerParams(dimension_semantics=("parallel", "parallel", "parallel")),
        name="gqa_lat",
    )(qkv, qkv, qkv, qkv, qkv, o_buf)


def _ctx_attn(qkv, o_buf):
    ctx_blk = T_LAT // LC
    grp = GQA_H // GQA_KV

    def qcol(h):
        return jnp.where(h < NA_H, COL_NA_Q + h, COL_G_Q + h - NA_H)

    def kcol(h):
        return jnp.where(h < NA_H, COL_NA_K + h, COL_G_K + (h - NA_H) // grp)

    def vcol(h):
        return jnp.where(h < NA_H, COL_NA_V + h, COL_G_V + (h - NA_H) // grp)

    return pl.pallas_call(
        functools.partial(_attn_kernel, two=False),
        out_shape=jax.ShapeDtypeStruct(o_buf.shape, o_buf.dtype),
        grid=(B, NA_H + GQA_H),
        in_specs=[pl.BlockSpec((LC, HD), lambda b, h: (ctx_blk + b, qcol(h))),
                  pl.BlockSpec((LC, HD), lambda b, h: (ctx_blk + b, kcol(h))),
                  pl.BlockSpec((LC, HD), lambda b, h: (ctx_blk + b, vcol(h))),
                  pl.BlockSpec(memory_space=pl.ANY)],
        out_specs=pl.BlockSpec((LC, HD), lambda b, h: (ctx_blk + b, h)),
        input_output_aliases={3: 0},
        compiler_params=pltpu.CompilerParams(dimension_semantics=("parallel", "parallel")),
        name="ctx_attn",
    )(qkv, qkv, qkv, o_buf)


def _na_kernel(q_ref, k_ref, v_ref, kc_ref, vc_ref, bias_ref, _, o_ref):
    kc = kc_ref[...]
    vc = vc_ref[...]
    n_win = NA_KH * GRID_W

    def body(r, carry):
        r0 = jnp.clip(r - NA_KH // 2, 0, ROWS - NA_KH)
        rel0 = r0 - r + NA_KH - 1
        q = q_ref[pl.ds(pl.multiple_of(r * GRID_W, GRID_W), GRID_W), :]
        kw = k_ref[pl.ds(pl.multiple_of(r0 * GRID_W, GRID_W), n_win), :]
        vw = v_ref[pl.ds(pl.multiple_of(r0 * GRID_W, GRID_W), n_win), :]
        sw = lax.dot_general(q, kw, _NT, preferred_element_type=F32) * ATT_SCALE + bias_ref[0, rel0]
        sc = lax.dot_general(q, kc, _NT, preferred_element_type=F32) * ATT_SCALE
        m = jnp.maximum(jnp.max(sw, axis=-1, keepdims=True), jnp.max(sc, axis=-1, keepdims=True))
        pw = jnp.exp(sw - m)
        pc = jnp.exp(sc - m)
        l = jnp.sum(pw, axis=-1, keepdims=True) + jnp.sum(pc, axis=-1, keepdims=True)
        o = (jnp.dot(pw.astype(BF16), vw, preferred_element_type=F32)
             + jnp.dot(pc.astype(BF16), vc, preferred_element_type=F32))
        o_ref[pl.ds(pl.multiple_of(r * GRID_W, GRID_W), GRID_W), :] = (o / l).astype(o_ref.dtype)
        return carry

    lax.fori_loop(0, ROWS, body, 0, unroll=4)


def _na_bias_table(rpb):
    n_rel = 2 * NA_KW - 1
    qc = np.arange(GRID_W)[:, None]
    kc = np.arange(GRID_W)[None, :]
    start = np.clip(qc - NA_KW // 2, 0, GRID_W - NA_KW)
    valid = (kc >= start) & (kc < start + NA_KW)
    onehot = ((np.arange(n_rel)[:, None, None] == (kc - qc + NA_KW - 1)[None]) & valid[None]).astype(np.float32)
    exp = jnp.einsum("hrc,cqk->hqrk", rpb, jnp.asarray(onehot), precision=lax.Precision.HIGHEST)
    exp = exp + jnp.asarray(np.where(valid, 0.0, NEG).astype(np.float32))[None, :, None, :]
    tabs = [exp[:, :, rel0:rel0 + NA_KH, :].reshape(NA_H, GRID_W, NA_KH * GRID_W) for rel0 in range(NA_KH)]
    return jnp.stack(tabs, axis=1)


def _na_lat(qkv, bias_tab, o_buf):
    ctx_blk = T_LAT // LC
    return pl.pallas_call(
        _na_kernel,
        out_shape=jax.ShapeDtypeStruct(o_buf.shape, o_buf.dtype),
        grid=(B, NA_H),
        in_specs=[pl.BlockSpec((S, HD), lambda b, h: (b, COL_NA_Q + h)),
                  pl.BlockSpec((S, HD), lambda b, h: (b, COL_NA_K + h)),
                  pl.BlockSpec((S, HD), lambda b, h: (b, COL_NA_V + h)),
                  pl.BlockSpec((LC, HD), lambda b, h: (ctx_blk + b, COL_NA_K + h)),
                  pl.BlockSpec((LC, HD), lambda b, h: (ctx_blk + b, COL_NA_V + h)),
                  pl.BlockSpec((1, NA_KH, GRID_W, NA_KH * GRID_W), lambda b, h: (h, 0, 0, 0)),
                  pl.BlockSpec(memory_space=pl.ANY)],
        out_specs=pl.BlockSpec((S, HD), lambda b, h: (b, h)),
        input_output_aliases={6: 0},
        compiler_params=pltpu.CompilerParams(dimension_semantics=("parallel", "parallel")),
        name="na_lat",
    )(qkv, qkv, qkv, qkv, qkv, bias_tab, o_buf)


def _gate_kernel(a_ref, w_ref, b_ref, o_ref):
    g = jnp.dot(a_ref[...], w_ref[...].astype(BF16), preferred_element_type=F32) + b_ref[...]
    g = GATE_SOFTCAP * jnp.tanh(g / GATE_SOFTCAP)
    lane = lax.broadcasted_iota(jnp.int32, g.shape, 1)
    is_forget = (lane // ML_H) % 2 == 1
    log_sig = jnp.minimum(g, 0.0) - jnp.log(1.0 + jnp.exp(-jnp.abs(g)))
    o_ref[...] = jnp.where(is_forget, log_sig, g)


def _mlstm_gates(h, wg, bg):
    tm = TM
    return pl.pallas_call(
        _gate_kernel,
        out_shape=jax.ShapeDtypeStruct((T_ALL, LANES), F32),
        grid=(T_ALL // tm,),
        in_specs=[pl.BlockSpec((tm, D), lambda i: (i, 0)),
                  pl.BlockSpec((D, LANES), lambda i: (0, 0)),
                  pl.BlockSpec((1, LANES), lambda i: (0, 0))],
        out_specs=pl.BlockSpec((tm, LANES), lambda i: (i, 0)),
        compiler_params=pltpu.CompilerParams(dimension_semantics=("parallel",)),
        name="mlstm_gates",
    )(h, wg, bg)


ML_SCALE = ML_DK ** -0.5
_TN = (((0,), (0,)), ((), ()))


def _mlstm_chunk(q, k, v, ig, fg, c_st, n_st, m_st, rev):
    L = q.shape[0]
    ti = lax.broadcasted_iota(jnp.int32, (L, L), 0)
    si = lax.broadcasted_iota(jnp.int32, (L, L), 1)
    eye = ti == si
    before_col = (si >= ti) if rev else (si <= ti)
    before_row = (ti >= si) if rev else (ti <= si)
    f_col = jnp.sum(jnp.where(eye, fg, 0.0), axis=1, keepdims=True)
    i_col = jnp.sum(jnp.where(eye, ig, 0.0), axis=1, keepdims=True)
    b_col = jnp.sum(jnp.where(before_col, fg, 0.0), axis=1, keepdims=True)
    b_row = jnp.sum(jnp.where(before_row, f_col, 0.0), axis=0, keepdims=True)
    total = jnp.sum(fg, axis=1, keepdims=True)
    g_col = total - b_col + i_col
    m_new = jnp.maximum(total + m_st, jnp.max(g_col, axis=0, keepdims=True))
    decay = jnp.exp(total + m_st - m_new)
    wk = jnp.exp(g_col - m_new)
    kw = k.astype(F32) * wk
    c_new = decay * c_st + lax.dot_general(kw.astype(BF16), v, _TN, preferred_element_type=F32)
    n_new = decay * n_st + jnp.sum(kw, axis=0, keepdims=True)

    dmat = jnp.where(before_col, b_col - b_row + ig, NEG)
    inter = b_col + m_st
    m_t = jnp.maximum(inter, jnp.max(dmat, axis=1, keepdims=True))
    a = jnp.exp(inter - m_t)
    qk = lax.dot_general(q, k, _NT, preferred_element_type=F32) * ML_SCALE
    smat = qk * jnp.exp(dmat - m_t)
    num = (a * (jnp.dot(q, c_st.astype(BF16), preferred_element_type=F32) * ML_SCALE)
           + jnp.dot(smat.astype(BF16), v, preferred_element_type=F32))
    den = (a * (jnp.sum(q.astype(F32) * n_st, axis=1, keepdims=True) * ML_SCALE)
           + jnp.sum(smat, axis=1, keepdims=True))
    h = num / jnp.maximum(jnp.abs(den), jnp.exp(-m_t))
    return h, c_new, n_new, m_new


def _mlstm_kernel(qf, kf, vf, gf, qb, kb, vb, gb, hf_ref, hb_ref, cf, nf, mf, cb, nb, mb):
    @pl.when(pl.program_id(1) == 0)
    def _():
        for r in (cf, nf, mf, cb, nb, mb):
            r[...] = jnp.zeros(r.shape, r.dtype)

    gfv = gf[0]
    h, c_new, n_new, m_new = _mlstm_chunk(qf[...], kf[...], vf[...], gfv[0:1], gfv[1:2],
                                          cf[...], nf[...], mf[...], False)
    hf_ref[...] = h
    cf[...] = c_new
    nf[...] = n_new
    mf[...] = m_new
    gbv = gb[0]
    h, c_new, n_new, m_new = _mlstm_chunk(qb[...], kb[...], vb[...], gbv[2:3], gbv[3:4],
                                          cb[...], nb[...], mb[...], True)
    hb_ref[...] = h
    cb[...] = c_new
    nb[...] = n_new
    mb[...] = m_new


def _mlstm_scan(proj, gates):
    L = ML_CHUNK
    n_lat = S // L
    assert LC == L
    ctx_blk = T_LAT // L
    qcol, kcol, vcol = 0, ML_H, (2 * ML_H * ML_DK) // ML_DV

    def fwd(bh, c):
        b = bh // ML_H
        return jnp.where(c == 0, ctx_blk + b, b * n_lat + c - 1)

    def bwd(bh, c):
        b = bh // ML_H
        return jnp.where(c == 0, ctx_blk + b, b * n_lat + n_lat - c)

    def specs(blk):
        return [pl.BlockSpec((L, ML_DK), lambda bh, c: (blk(bh, c), qcol + bh % ML_H)),
                pl.BlockSpec((L, ML_DK), lambda bh, c: (blk(bh, c), kcol + bh % ML_H)),
                pl.BlockSpec((L, ML_DV), lambda bh, c: (blk(bh, c), vcol + bh % ML_H)),
                pl.BlockSpec((1, 4, L), lambda bh, c: (bh % ML_H, 0, blk(bh, c)))]

    out_sds = jax.ShapeDtypeStruct((T_ALL, ML_H * ML_DV), F32)
    return pl.pallas_call(
        _mlstm_kernel,
        out_shape=[out_sds, out_sds],
        grid=(B * ML_H, 1 + n_lat),
        in_specs=specs(fwd) + specs(bwd),
        out_specs=[pl.BlockSpec((L, ML_DV), lambda bh, c: (fwd(bh, c), bh % ML_H)),
                   pl.BlockSpec((L, ML_DV), lambda bh, c: (bwd(bh, c), bh % ML_H))],
        scratch_shapes=[pltpu.VMEM((ML_DK, ML_DV), F32), pltpu.VMEM((1, ML_DK), F32), pltpu.VMEM((1, 1), F32),
                        pltpu.VMEM((ML_DK, ML_DV), F32), pltpu.VMEM((1, ML_DK), F32), pltpu.VMEM((1, 1), F32)],
        compiler_params=pltpu.CompilerParams(dimension_semantics=("parallel", "arbitrary")),
        name="mlstm_scan",
    )(proj, proj, proj, gates, proj, proj, proj, gates)


def _readout_kernel(hf_ref, hb_ref, o_ref, gain_ref, out_ref):
    hs = hf_ref[...] + hb_ref[...]
    o = o_ref[...].astype(F32)
    for hh in range(ML_H):
        sl = slice(hh * ML_DV, (hh + 1) * ML_DV)
        x = hs[:, sl]
        ms = jnp.mean(x * x, axis=-1, keepdims=True)
        y = x * lax.rsqrt(ms + NORM_EPS) * gain_ref[:, sl]
        og = o[:, sl]
        out_ref[:, sl] = (y * (1.0 / (1.0 + jnp.exp(-og)))).astype(out_ref.dtype)


def _mlstm_readout(hf, hb, proj, gain, n_rows):
    tm = TNORM
    wide = ML_H * ML_DV
    ocol = (2 * ML_H * ML_DK + ML_H * ML_DV) // wide
    return pl.pallas_call(
        _readout_kernel,
        out_shape=jax.ShapeDtypeStruct((n_rows, wide), BF16),
        grid=(n_rows // tm,),
        in_specs=[pl.BlockSpec((tm, wide), lambda i: (i, 0)),
                  pl.BlockSpec((tm, wide), lambda i: (i, 0)),
                  pl.BlockSpec((tm, wide), lambda i: (i, ocol)),
                  pl.BlockSpec((1, wide), lambda i: (0, 0))],
        out_specs=pl.BlockSpec((tm, wide), lambda i: (i, 0)),
        compiler_params=pltpu.CompilerParams(dimension_semantics=("parallel",)),
        name="mlstm_readout",
    )(hf, hb, proj, gain)


def _moe_kernel(be_ref, nu_ref, tok_ref, h_hbm, w1_ref, b1_ref, w2_ref, b2_ref, sel_ref, y_ref,
                xf_ref, xb_ref, sem):
    i = pl.program_id(0)
    f = pl.program_id(1)
    bm = xb_ref.shape[0]
    n_used = nu_ref[0]

    def row_copy(blk, r):
        tok = tok_ref[blk * bm + r]
        return pltpu.make_async_copy(h_hbm.at[pl.ds(tok, 1)], xf_ref.at[pl.ds(r, 1)], sem)

    def start_block(blk):
        def body(r, carry):
            row_copy(blk, r).start()
            return carry
        lax.fori_loop(0, bm, body, 0, unroll=8)

    def wait_block(blk):
        def body(r, carry):
            row_copy(blk, r).wait()
            return carry
        lax.fori_loop(0, bm, body, 0, unroll=8)

    @pl.when(i < n_used)
    def _():
        @pl.when(jnp.logical_and(i == 0, f == 0))
        def _():
            start_block(0)

        @pl.when(f == 0)
        def _():
            wait_block(i)
            xb_ref[...] = xf_ref[...].astype(BF16)

        @pl.when(jnp.logical_and(f == 1, i + 1 < n_used))
        def _():
            start_block(i + 1)

        h = jnp.dot(xb_ref[...], w1_ref[0].astype(BF16), preferred_element_type=F32) + b1_ref[0]
        glu = jnp.minimum(h, SWIGLU_LIMIT)
        glu = glu * (1.0 / (1.0 + jnp.exp(-SWIGLU_ALPHA * glu)))
        lin = jnp.clip(h, -SWIGLU_LIMIT, SWIGLU_LIMIT) + 1.0
        prod = glu * pltpu.roll(lin, h.shape[1] - 1, 1)
        act = jnp.dot(prod.astype(BF16), sel_ref[...], preferred_element_type=F32)
        part = jnp.dot(act.astype(BF16), w2_ref[0].astype(BF16), preferred_element_type=F32)

        @pl.when(f == 0)
        def _():
            y_ref[...] = part + b2_ref[0]

        @pl.when(f != 0)
        def _():
            y_ref[...] += part

    @pl.when(jnp.logical_and(i >= n_used, f == 0))
    def _():
        y_ref[...] = jnp.zeros(y_ref.shape, y_ref.dtype)


def _moe_blocks_max(n_tok):
    m = n_tok * TOP_K
    return -(-(m + N_EXP * (MOE_BM - 1)) // MOE_BM)


def _moe_experts(h, row_tok, block_expert, n_used, layer, w1, b1, w2, b2, sel):
    bm, fh = MOE_BM, MOE_FH
    n_blocks = row_tok.shape[0] // bm
    nf = D_EXP // fh
    e0 = layer * N_EXP

    def blk(i, nu):
        return jnp.minimum(i, nu[0] - 1)

    grid_spec = pltpu.PrefetchScalarGridSpec(
        num_scalar_prefetch=3,
        grid=(n_blocks, nf),
        in_specs=[pl.BlockSpec(memory_space=pl.ANY),
                  pl.BlockSpec((1, D, 2 * fh), lambda i, f, be, nu, tok: (e0 + be[blk(i, nu)], 0, f)),
                  pl.BlockSpec((1, 1, 2 * fh), lambda i, f, be, nu, tok: (be[blk(i, nu)], 0, f)),
                  pl.BlockSpec((1, fh, D), lambda i, f, be, nu, tok: (e0 + be[blk(i, nu)], f, 0)),
                  pl.BlockSpec((1, 1, D), lambda i, f, be, nu, tok: (be[blk(i, nu)], 0, 0)),
                  pl.BlockSpec((2 * fh, fh), lambda i, f, be, nu, tok: (0, 0))],
        out_specs=pl.BlockSpec((bm, D), lambda i, f, be, nu, tok: (i, 0)),
        scratch_shapes=[pltpu.VMEM((bm, D), F32), pltpu.VMEM((bm, D), BF16), pltpu.SemaphoreType.DMA],
    )
    return pl.pallas_call(
        _moe_kernel,
        out_shape=jax.ShapeDtypeStruct((n_blocks * bm, D), F32),
        grid_spec=grid_spec,
        compiler_params=pltpu.CompilerParams(dimension_semantics=("arbitrary", "arbitrary")),
        name="moe_experts",
    )(block_expert, n_used, row_tok, h, w1, b1, w2, b2, sel)


def _combine_kernel(dest_ref, x_ref, y_hbm, p_ref, gate_ref, o_ref, buf, sem):
    i = pl.program_id(0)
    tm = x_ref.shape[0]

    def row_copy(kk, r):
        d = dest_ref[(i * tm + r) * TOP_K + kk]
        return pltpu.make_async_copy(y_hbm.at[pl.ds(d, 1)], buf.at[kk, pl.ds(r, 1)], sem.at[kk])

    for kk in range(TOP_K):
        def start(r, carry, kk=kk):
            row_copy(kk, r).start()
            return carry
        lax.fori_loop(0, tm, start, 0, unroll=8)

    p = p_ref[...]
    acc = None
    for kk in range(TOP_K):
        def wait(r, carry, kk=kk):
            row_copy(kk, r).wait()
            return carry
        lax.fori_loop(0, tm, wait, 0, unroll=8)
        term = p[:, kk:kk + 1] * buf[kk]
        acc = term if acc is None else acc + term
    o_ref[...] = x_ref[...] + gate_ref[0] * acc


def _moe_combine(xs, y_rows, dest, probs, mod, gate_chunk, n_rows):
    tm = 256
    grid_spec = pltpu.PrefetchScalarGridSpec(
        num_scalar_prefetch=1,
        grid=(n_rows // tm,),
        in_specs=[pl.BlockSpec((tm, D), lambda i, dst: (i, 0)),
                  pl.BlockSpec(memory_space=pl.ANY),
                  pl.BlockSpec((tm, LANES), lambda i, dst: (i, 0)),
                  pl.BlockSpec((1, 1, D), lambda i, dst: (_mod_row(i, tm), 0, gate_chunk))],
        out_specs=pl.BlockSpec((tm, D), lambda i, dst: (i, 0)),
        scratch_shapes=[pltpu.VMEM((TOP_K, tm, D), F32), pltpu.SemaphoreType.DMA((TOP_K,))],
    )
    return pl.pallas_call(
        _combine_kernel,
        out_shape=jax.ShapeDtypeStruct((n_rows, D), F32),
        grid_spec=grid_spec,
        compiler_params=pltpu.CompilerParams(dimension_semantics=("arbitrary",)),
        name="moe_combine",
    )(dest, xs, y_rows, probs, mod)


def _moe_route(top_e, n_tok):
    m = n_tok * TOP_K
    bm = MOE_BM
    n_blocks = _moe_blocks_max(n_tok)
    e_flat = top_e.reshape(m)
    onehot = (e_flat[:, None] == jnp.arange(N_EXP, dtype=jnp.int32)[None, :]).astype(jnp.int32)
    csum = jnp.cumsum(onehot, axis=0)
    counts = csum[-1]
    rank = jnp.take_along_axis(csum, e_flat[:, None], axis=1)[:, 0] - 1
    padded = (counts + bm - 1) // bm * bm
    pad_end = jnp.cumsum(padded)
    pad_start = pad_end - padded
    dest = pad_start[e_flat] + rank
    row_tok = jnp.zeros((n_blocks * bm,), jnp.int32).at[dest].set(jnp.arange(m, dtype=jnp.int32) // TOP_K)
    n_used = (pad_end[-1] // bm).astype(jnp.int32).reshape(1)
    starts = jnp.arange(n_blocks, dtype=jnp.int32) * bm
    block_expert = jnp.minimum(jnp.sum((pad_end[None, :] <= starts[:, None]).astype(jnp.int32), axis=1),
                               N_EXP - 1).astype(jnp.int32)
    return dest, row_tok, block_expert, n_used


def _moe_layer(xs, n_rows, g, mod, router_w, router_b, layer, w1_all, b1, w2_all, b2, sel, name):
    rw = jnp.pad(router_w, ((0, 0), (0, LANES - N_EXP)))
    rb = jnp.pad(router_b, (0, LANES - N_EXP), constant_values=NEG).reshape(1, LANES)
    h, top_e, probs = _norm_mod(xs, g, mod, 3, 4, n_rows=n_rows, router_w=rw, router_b=rb, name=name + "_norm")
    dest, row_tok, block_expert, n_used = _moe_route(top_e[:, :TOP_K], n_rows)
    y_rows = _moe_experts(h, row_tok, block_expert, n_used, layer, w1_all, b1.reshape(N_EXP, 1, 2 * D_EXP),
                          w2_all, b2.reshape(N_EXP, 1, D), sel)
    return _moe_combine(xs, y_rows, dest, probs, mod, 5, n_rows)


def _select_matrix():
    r = np.arange(2 * MOE_FH)[:, None]
    c = np.arange(MOE_FH)[None, :]
    return jnp.asarray(r == 2 * c, BF16)


def kernel(x, c, ctx, c_ctx, ada_w, ada_b, norm_g, attn_w_in, attn_w_out, attn_qk_gain, na_rpb, mlstm_w_in,
           mlstm_gate_bias, mlstm_head_gain, mlstm_w_out, router_w, router_b, expert_w1, expert_b1,
           expert_w2, expert_b2):
    xs = jnp.concatenate([x.reshape(T_LAT, D), ctx.reshape(T_CTX, D)], axis=0)
    cc = jnp.zeros((8, D), F32).at[:B].set(c).at[MOD_CTX_ROW].set(c_ctx)
    sel = _select_matrix()

    n_layers = ada_w.shape[0]
    ada_w2 = ada_w.reshape(n_layers * D, 6 * D)
    w1_all = expert_w1.reshape(n_layers * N_EXP, D, 2 * D_EXP)
    w2_all = expert_w2.reshape(n_layers * N_EXP, D_EXP, D)

    def ada(layer):
        m = _matmul(cc, ada_w2, 6 * D, out_dtype=F32, tm=8, w_row_block=layer, silu=True,
                    bias=ada_b[layer].reshape(1, 6 * D), name="adaln")
        return m.reshape(8, 1, 6 * D)

    mod = ada(0)
    h = _norm_mod(xs, norm_g[0, 0].reshape(1, D), mod, 0, 1, n_rows=T_ALL, name="l0_norm1")
    qkv = _matmul(h, attn_w_in[0], ATTN_IN, out_dtype=BF16, name="attn_in")
    cos_rep, sin_signed = _rope_tables()
    qkv = _qk_prep(qkv, attn_qk_gain[0].reshape(4, 1, HD), cos_rep, sin_signed)
    o_buf = _na_lat(qkv, _na_bias_table(na_rpb[0]), jnp.zeros((T_ALL, D), BF16))
    o_buf = _gqa_lat(qkv, o_buf)
    o_buf = _ctx_attn(qkv, o_buf)
    xs = _matmul(o_buf, attn_w_out[0], D, out_dtype=F32, res=xs, mod=mod, gate_chunk=2, name="attn_out")
    xs = _moe_layer(xs, T_ALL, norm_g[0, 1].reshape(1, D), mod, router_w[0], router_b[0],
                    0, w1_all, expert_b1[0], w2_all, expert_b2[0], sel, "l0_moe")

    mod = ada(1)
    h = _norm_mod(xs, norm_g[1, 0].reshape(1, D), mod, 0, 1, n_rows=T_ALL, name="l1_norm1")
    proj = _matmul(h, mlstm_w_in[0], ML_MAIN, out_dtype=BF16, name="mlstm_in")
    wg = jnp.pad(mlstm_w_in[0][:, ML_MAIN:], ((0, 0), (0, LANES - 4 * ML_H)))
    bg = jnp.pad(mlstm_gate_bias[0].reshape(-1), (0, LANES - 4 * ML_H)).reshape(1, LANES)
    g = _mlstm_gates(h, wg, bg)
    gates = g[:, :4 * ML_H].T.reshape(4, ML_H, T_ALL).transpose(1, 0, 2)
    hf, hb = _mlstm_scan(proj, gates)
    hn = _mlstm_readout(hf, hb, proj, mlstm_head_gain[0].reshape(1, ML_H * ML_DV), T_LAT)
    xs = _matmul(hn, mlstm_w_out[0], D, out_dtype=F32, m_rows=T_LAT, res=xs, mod=mod, gate_chunk=2,
                 name="mlstm_out")
    out = _moe_layer(xs, T_LAT, norm_g[1, 1].reshape(1, D), mod, router_w[1], router_b[1],
                     1, w1_all, expert_b1[1], w2_all, expert_b2[1], sel, "l1_moe")
    return out.reshape(B, S, D)
```

```python
import functools

import jax
import jax.numpy as jnp
import numpy as np
from jax import lax
from jax.experimental import pallas as pl
from jax.experimental.pallas import tpu as pltpu

F32 = jnp.float32
BF16 = jnp.bfloat16

D = 2048
B = 4
S = 2048
LC = 256
GRID_W = 64
ROWS = S // GRID_W
HD = 128
NA_H = 8
GQA_H = 8
GQA_KV = 2
NA_KH = 8
NA_KW = 16
ROPE_THETA = 10000.0
ML_H = 8
ML_DK = 128
ML_DV = 256
GATE_SOFTCAP = 15.0
N_EXP = 32
TOP_K = 4
D_EXP = D
SWIGLU_ALPHA = 1.702
SWIGLU_LIMIT = 7.0
NORM_EPS = 1e-6
ATTN_IN = 3 * NA_H * HD + GQA_H * HD + 2 * GQA_KV * HD
ML_MAIN = 2 * ML_H * ML_DK + 2 * ML_H * ML_DV

T_LAT = B * S
T_CTX = B * LC
T_ALL = T_LAT + T_CTX
MOD_CTX_ROW = B

LANES = 128

TM = 1024
TN = 512
TNORM = 512
ML_CHUNK = 256
MOE_BM = 512
MOE_FH = 512
MOE_NF = D_EXP // MOE_FH
MOE_SUB = 128
NEG = -1e30


def _mod_row(i, tm):
    n_lat = T_LAT // tm
    per_b = S // tm
    return jnp.where(i < n_lat, i // per_b, MOD_CTX_ROW)


def _mm_kernel(*refs, silu, has_bias, has_res):
    a_ref, w_ref = refs[0], refs[1]
    pos = 2
    bias_ref = res_ref = gate_ref = None
    if has_bias:
        bias_ref = refs[pos]
        pos += 1
    if has_res:
        res_ref, gate_ref = refs[pos], refs[pos + 1]
        pos += 2
    o_ref = refs[pos]
    a = a_ref[...]
    if silu:
        a = a.astype(F32)
        a = a * (1.0 / (1.0 + jnp.exp(-a)))
    acc = jnp.dot(a.astype(BF16), w_ref[...].astype(BF16), preferred_element_type=F32)
    if has_bias:
        acc = acc + bias_ref[...]
    if has_res:
        acc = res_ref[...] + gate_ref[0] * acc
    o_ref[...] = acc.astype(o_ref.dtype)


def _matmul(a, w, n_out, *, out_dtype, tm=TM, tn=TN, m_rows=None, w_row_block=0, silu=False, bias=None,
            res=None, mod=None, gate_chunk=None, name="mm"):
    m_rows = a.shape[0] if m_rows is None else m_rows
    k = a.shape[1]
    grid = (m_rows // tm, n_out // tn)
    in_specs = [pl.BlockSpec((tm, k), lambda i, j: (i, 0)),
                pl.BlockSpec((k, tn), lambda i, j: (w_row_block, j))]
    args = [a, w]
    if bias is not None:
        in_specs.append(pl.BlockSpec((1, tn), lambda i, j: (0, j)))
        args.append(bias)
    if res is not None:
        per = D // tn
        in_specs.append(pl.BlockSpec((tm, tn), lambda i, j: (i, j)))
        in_specs.append(pl.BlockSpec((1, 1, tn), lambda i, j: (_mod_row(i, tm), 0, gate_chunk * per + j)))
        args += [res, mod]
    return pl.pallas_call(
        functools.partial(_mm_kernel, silu=silu, has_bias=bias is not None, has_res=res is not None),
        out_shape=jax.ShapeDtypeStruct((m_rows, n_out), out_dtype),
        grid=grid, in_specs=in_specs,
        out_specs=pl.BlockSpec((tm, tn), lambda i, j: (i, j)),
        compiler_params=pltpu.CompilerParams(dimension_semantics=("parallel", "parallel")),
        name=name,
    )(*args)


def _norm_kernel(*refs, router):
    x_ref, g_ref, shift_ref, scale_ref = refs[:4]
    x = x_ref[...]
    ms = jnp.mean(x * x, axis=-1, keepdims=True)
    y = x * lax.rsqrt(ms + NORM_EPS) * g_ref[...]
    h = y * (1.0 + scale_ref[0]) + shift_ref[0]
    hb = h.astype(BF16)
    if not router:
        refs[4][...] = hb
        return
    rw_ref, rb_ref, h_ref, e_ref, p_ref = refs[4:]
    h_ref[...] = h
    logits = jnp.dot(hb, rw_ref[...].astype(BF16), preferred_element_type=F32) + rb_ref[...]
    lane = lax.broadcasted_iota(jnp.int32, logits.shape, 1)
    e_out = jnp.zeros(logits.shape, jnp.int32)
    v_out = jnp.full(logits.shape, NEG, F32)
    work = logits
    for kk in range(TOP_K):
        mx = jnp.max(work, axis=-1, keepdims=True)
        idx = jnp.min(jnp.where(work == mx, lane, LANES), axis=-1, keepdims=True)
        e_out = jnp.where(lane == kk, idx, e_out)
        v_out = jnp.where(lane == kk, mx, v_out)
        work = jnp.where(lane == idx, -jnp.inf, work)
    top0 = jnp.max(v_out, axis=-1, keepdims=True)
    pe = jnp.exp(v_out - top0)
    e_ref[...] = e_out
    p_ref[...] = pe / jnp.sum(pe, axis=-1, keepdims=True)


def _norm_mod(xs, g, mod, shift_chunk, scale_chunk, *, n_rows, router_w=None, router_b=None, name="norm"):
    tm = TNORM
    in_specs = [pl.BlockSpec((tm, D), lambda i: (i, 0)),
                pl.BlockSpec((1, D), lambda i: (0, 0)),
                pl.BlockSpec((1, 1, D), lambda i: (_mod_row(i, tm), 0, shift_chunk)),
                pl.BlockSpec((1, 1, D), lambda i: (_mod_row(i, tm), 0, scale_chunk))]
    args = [xs, g, mod, mod]
    router = router_w is not None
    out_shape = [jax.ShapeDtypeStruct((n_rows, D), F32 if router else BF16)]
    out_specs = [pl.BlockSpec((tm, D), lambda i: (i, 0))]
    if router:
        in_specs += [pl.BlockSpec((D, LANES), lambda i: (0, 0)),
                     pl.BlockSpec((1, LANES), lambda i: (0, 0))]
        args += [router_w, router_b]
        out_shape += [jax.ShapeDtypeStruct((n_rows, LANES), jnp.int32),
                      jax.ShapeDtypeStruct((n_rows, LANES), F32)]
        out_specs += [pl.BlockSpec((tm, LANES), lambda i: (i, 0))] * 2
    out = pl.pallas_call(
        functools.partial(_norm_kernel, router=router),
        out_shape=out_shape, grid=(n_rows // tm,), in_specs=in_specs, out_specs=out_specs,
        compiler_params=pltpu.CompilerParams(dimension_semantics=("parallel",)),
        name=name,
    )(*args)
    return out if router else out[0]


QK_PAIR = 2
N_QK_SLOTS = (2 * NA_H + GQA_H + GQA_KV) // QK_PAIR
QK_NA_SLOTS = 2 * NA_H // QK_PAIR


def _qk_col(j):
    return jnp.where(j < QK_NA_SLOTS, j, j + NA_H // QK_PAIR)


def _qk_gain_row(j):
    half = NA_H // QK_PAIR
    return jnp.where(j < half, 0, jnp.where(j < 2 * half, 1, jnp.where(j < 3 * half, 2, 3)))


def _qk_kernel(x_ref, gain_ref, cos_ref, sin_ref, o_ref, *, tm):
    i = pl.program_id(0)
    j = pl.program_id(1)
    do_rope = jnp.logical_and(j >= QK_NA_SLOTS, i < T_LAT // tm)
    for hh in range(QK_PAIR):
        sl = slice(hh * HD, (hh + 1) * HD)
        y = x_ref[:, sl].astype(F32)
        ms = jnp.mean(y * y, axis=-1, keepdims=True)
        yn = y * lax.rsqrt(ms + NORM_EPS) * gain_ref[0]
        lane = lax.broadcasted_iota(jnp.int32, yn.shape, 1)
        nxt = pltpu.roll(yn, LANES - 1, 1)
        prv = pltpu.roll(yn, 1, 1)
        partner = jnp.where(lane % 2 == 0, nxt, prv)
        yr = yn * cos_ref[...] + partner * sin_ref[...]
        o_ref[:, sl] = jnp.where(do_rope, yr, yn).astype(o_ref.dtype)


def _qk_prep(qkv, gain, cos_rep, sin_signed):
    tm = TM
    per_b = S // tm
    wide = QK_PAIR * HD
    return pl.pallas_call(
        functools.partial(_qk_kernel, tm=tm),
        out_shape=jax.ShapeDtypeStruct(qkv.shape, qkv.dtype),
        grid=(T_ALL // tm, N_QK_SLOTS),
        in_specs=[pl.BlockSpec((tm, wide), lambda i, j: (i, _qk_col(j))),
                  pl.BlockSpec((1, 1, HD), lambda i, j: (_qk_gain_row(j), 0, 0)),
                  pl.BlockSpec((tm, HD), lambda i, j: (i % per_b, 0)),
                  pl.BlockSpec((tm, HD), lambda i, j: (i % per_b, 0))],
        out_specs=pl.BlockSpec((tm, wide), lambda i, j: (i, _qk_col(j))),
        input_output_aliases={0: 0},
        compiler_params=pltpu.CompilerParams(dimension_semantics=("parallel", "parallel")),
        name="qk_prep",
    )(qkv, gain, cos_rep, sin_signed)


def _rope_tables():
    t = np.arange(S)
    row = (t // GRID_W).astype(np.float32)
    col = (t % GRID_W).astype(np.float32)
    axis_dim = HD // 2
    inv_freq = jnp.asarray(ROPE_THETA, F32) ** (-jnp.arange(0, axis_dim, 2, dtype=F32) / axis_dim)
    ang = jnp.concatenate([jnp.asarray(row)[:, None] * inv_freq, jnp.asarray(col)[:, None] * inv_freq], axis=-1)
    cos, sin = jnp.cos(ang), jnp.sin(ang)
    cos_rep = jnp.repeat(cos, 2, axis=-1)
    sin_signed = jnp.stack([-sin, sin], axis=-1).reshape(S, HD)
    return cos_rep, sin_signed


COL_NA_Q, COL_NA_K, COL_NA_V = 0, NA_H, 2 * NA_H
COL_G_Q, COL_G_K, COL_G_V = 3 * NA_H, 3 * NA_H + GQA_H, 3 * NA_H + GQA_H + GQA_KV
ATT_SCALE = HD ** -0.5
_NT = (((1,), (1,)), ((), ()))


def _attn_kernel(*refs, two):
    if two:
        q_ref, k1_ref, v1_ref, k2_ref, v2_ref, _, o_ref = refs
    else:
        q_ref, k1_ref, v1_ref, _, o_ref = refs
    q = q_ref[...]
    s1 = lax.dot_general(q, k1_ref[...], _NT, preferred_element_type=F32) * ATT_SCALE
    m = jnp.max(s1, axis=-1, keepdims=True)
    if two:
        s2 = lax.dot_general(q, k2_ref[...], _NT, preferred_element_type=F32) * ATT_SCALE
        m = jnp.maximum(m, jnp.max(s2, axis=-1, keepdims=True))
    p1 = jnp.exp(s1 - m)
    l = jnp.sum(p1, axis=-1, keepdims=True)
    o = jnp.dot(p1.astype(BF16), v1_ref[...], preferred_element_type=F32)
    if two:
        p2 = jnp.exp(s2 - m)
        l = l + jnp.sum(p2, axis=-1, keepdims=True)
        o = o + jnp.dot(p2.astype(BF16), v2_ref[...], preferred_element_type=F32)
    o_ref[...] = (o / l).astype(o_ref.dtype)


def _gqa_lat(qkv, o_buf):
    tq = 512
    nq = S // tq
    ctx_blk = T_LAT // LC
    grp = GQA_H // GQA_KV
    return pl.pallas_call(
        functools.partial(_attn_kernel, two=True),
        out_shape=jax.ShapeDtypeStruct(o_buf.shape, o_buf.dtype),
        grid=(B, GQA_H, nq),
        in_specs=[pl.BlockSpec((tq, HD), lambda b, h, t: (b * nq + t, COL_G_Q + h)),
                  pl.BlockSpec((LC, HD), lambda b, h, t: (ctx_blk + b, COL_G_K + h // grp)),
                  pl.BlockSpec((LC, HD), lambda b, h, t: (ctx_blk + b, COL_G_V + h // grp)),
                  pl.BlockSpec((S, HD), lambda b, h, t: (b, COL_G_K + h // grp)),
                  pl.BlockSpec((S, HD), lambda b, h, t: (b, COL_G_V + h // grp)),
                  pl.BlockSpec(memory_space=pl.ANY)],
        out_specs=pl.BlockSpec((tq, HD), lambda b, h, t: (b * nq + t, NA_H + h)),
        input_output_aliases={5: 0},
        compiler_params=pltpu.CompilerParams(dimension_semantics=("parallel", "parallel", "parallel")),
        name="gqa_lat",
    )(qkv, qkv, qkv, qkv, qkv, o_buf)


def _ctx_attn(qkv, o_buf):
    ctx_blk = T_LAT // LC
    grp = GQA_H // GQA_KV

    def qcol(h):
        return jnp.where(h < NA_H, COL_NA_Q + h, COL_G_Q + h - NA_H)

    def kcol(h):
        return jnp.where(h < NA_H, COL_NA_K + h, COL_G_K + (h - NA_H) // grp)

    def vcol(h):
        return jnp.where(h < NA_H, COL_NA_V + h, COL_G_V + (h - NA_H) // grp)

    return pl.pallas_call(
        functools.partial(_attn_kernel, two=False),
        out_shape=jax.ShapeDtypeStruct(o_buf.shape, o_buf.dtype),
        grid=(B, NA_H + GQA_H),
        in_specs=[pl.BlockSpec((LC, HD), lambda b, h: (ctx_blk + b, qcol(h))),
                  pl.BlockSpec((LC, HD), lambda b, h: (ctx_blk + b, kcol(h))),
                  pl.BlockSpec((LC, HD), lambda b, h: (ctx_blk + b, vcol(h))),
                  pl.BlockSpec(memory_space=pl.ANY)],
        out_specs=pl.BlockSpec((LC, HD), lambda b, h: (ctx_blk + b, h)),
        input_output_aliases={3: 0},
        compiler_params=pltpu.CompilerParams(dimension_semantics=("parallel", "parallel")),
        name="ctx_attn",
    )(qkv, qkv, qkv, o_buf)


def _na_kernel(q_ref, k_ref, v_ref, kc_ref, vc_ref, bias_ref, _, o_ref):
    kc = kc_ref[...]
    vc = vc_ref[...]
    n_win = NA_KH * GRID_W

    def body(r, carry):
        r0 = jnp.clip(r - NA_KH // 2, 0, ROWS - NA_KH)
        rel0 = r0 - r + NA_KH - 1
        q = q_ref[pl.ds(pl.multiple_of(r * GRID_W, GRID_W), GRID_W), :]
        kw = k_ref[pl.ds(pl.multiple_of(r0 * GRID_W, GRID_W), n_win), :]
        vw = v_ref[pl.ds(pl.multiple_of(r0 * GRID_W, GRID_W), n_win), :]
        sw = lax.dot_general(q, kw, _NT, preferred_element_type=F32) * ATT_SCALE + bias_ref[0, rel0]
        sc = lax.dot_general(q, kc, _NT, preferred_element_type=F32) * ATT_SCALE
        m = jnp.maximum(jnp.max(sw, axis=-1, keepdims=True), jnp.max(sc, axis=-1, keepdims=True))
        pw = jnp.exp(sw - m)
        pc = jnp.exp(sc - m)
        l = jnp.sum(pw, axis=-1, keepdims=True) + jnp.sum(pc, axis=-1, keepdims=True)
        o = (jnp.dot(pw.astype(BF16), vw, preferred_element_type=F32)
             + jnp.dot(pc.astype(BF16), vc, preferred_element_type=F32))
        o_ref[pl.ds(pl.multiple_of(r * GRID_W, GRID_W), GRID_W), :] = (o / l).astype(o_ref.dtype)
        return carry

    lax.fori_loop(0, ROWS, body, 0, unroll=4)


def _na_bias_table(rpb):
    n_rel = 2 * NA_KW - 1
    qc = np.arange(GRID_W)[:, None]
    kc = np.arange(GRID_W)[None, :]
    start = np.clip(qc - NA_KW // 2, 0, GRID_W - NA_KW)
    valid = (kc >= start) & (kc < start + NA_KW)
    onehot = ((np.arange(n_rel)[:, None, None] == (kc - qc + NA_KW - 1)[None]) & valid[None]).astype(np.float32)
    exp = jnp.einsum("hrc,cqk->hqrk", rpb, jnp.asarray(onehot), precision=lax.Precision.HIGHEST)
    exp = exp + jnp.asarray(np.where(valid, 0.0, NEG).astype(np.float32))[None, :, None, :]
    tabs = [exp[:, :, rel0:rel0 + NA_KH, :].reshape(NA_H, GRID_W, NA_KH * GRID_W) for rel0 in range(NA_KH)]
    return jnp.stack(tabs, axis=1)


def _na_lat(qkv, bias_tab, o_buf):
    ctx_blk = T_LAT // LC
    return pl.pallas_call(
        _na_kernel,
        out_shape=jax.ShapeDtypeStruct(o_buf.shape, o_buf.dtype),
        grid=(B, NA_H),
        in_specs=[pl.BlockSpec((S, HD), lambda b, h: (b, COL_NA_Q + h)),
                  pl.BlockSpec((S, HD), lambda b, h: (b, COL_NA_K + h)),
                  pl.BlockSpec((S, HD), lambda b, h: (b, COL_NA_V + h)),
                  pl.BlockSpec((LC, HD), lambda b, h: (ctx_blk + b, COL_NA_K + h)),
                  pl.BlockSpec((LC, HD), lambda b, h: (ctx_blk + b, COL_NA_V + h)),
                  pl.BlockSpec((1, NA_KH, GRID_W, NA_KH * GRID_W), lambda b, h: (h, 0, 0, 0)),
                  pl.BlockSpec(memory_space=pl.ANY)],
        out_specs=pl.BlockSpec((S, HD), lambda b, h: (b, h)),
        input_output_aliases={6: 0},
        compiler_params=pltpu.CompilerParams(dimension_semantics=("parallel", "parallel")),
        name="na_lat",
    )(qkv, qkv, qkv, qkv, qkv, bias_tab, o_buf)


def _gate_kernel(a_ref, w_ref, b_ref, o_ref):
    g = jnp.dot(a_ref[...], w_ref[...].astype(BF16), preferred_element_type=F32) + b_ref[...]
    g = GATE_SOFTCAP * jnp.tanh(g / GATE_SOFTCAP)
    lane = lax.broadcasted_iota(jnp.int32, g.shape, 1)
    is_forget = (lane // ML_H) % 2 == 1
    log_sig = jnp.minimum(g, 0.0) - jnp.log(1.0 + jnp.exp(-jnp.abs(g)))
    o_ref[...] = jnp.where(is_forget, log_sig, g)


def _mlstm_gates(h, wg, bg):
    tm = TM
    return pl.pallas_call(
        _gate_kernel,
        out_shape=jax.ShapeDtypeStruct((T_ALL, LANES), F32),
        grid=(T_ALL // tm,),
        in_specs=[pl.BlockSpec((tm, D), lambda i: (i, 0)),
                  pl.BlockSpec((D, LANES), lambda i: (0, 0)),
                  pl.BlockSpec((1, LANES), lambda i: (0, 0))],
        out_specs=pl.BlockSpec((tm, LANES), lambda i: (i, 0)),
        compiler_params=pltpu.CompilerParams(dimension_semantics=("parallel",)),
        name="mlstm_gates",
    )(h, wg, bg)


ML_SCALE = ML_DK ** -0.5
_TN = (((0,), (0,)), ((), ()))


def _mlstm_chunk(q, k, v, ig, fg, c_st, n_st, m_st, rev):
    L = q.shape[0]
    ti = lax.broadcasted_iota(jnp.int32, (L, L), 0)
    si = lax.broadcasted_iota(jnp.int32, (L, L), 1)
    eye = ti == si
    before_col = (si >= ti) if rev else (si <= ti)
    before_row = (ti >= si) if rev else (ti <= si)
    f_col = jnp.sum(jnp.where(eye, fg, 0.0), axis=1, keepdims=True)
    i_col = jnp.sum(jnp.where(eye, ig, 0.0), axis=1, keepdims=True)
    b_col = jnp.sum(jnp.where(before_col, fg, 0.0), axis=1, keepdims=True)
    b_row = jnp.sum(jnp.where(before_row, f_col, 0.0), axis=0, keepdims=True)
    total = jnp.sum(fg, axis=1, keepdims=True)
    g_col = total - b_col + i_col
    m_new = jnp.maximum(total + m_st, jnp.max(g_col, axis=0, keepdims=True))
    decay = jnp.exp(total + m_st - m_new)
    wk = jnp.exp(g_col - m_new)
    kw = k.astype(F32) * wk
    c_new = decay * c_st + lax.dot_general(kw.astype(BF16), v, _TN, preferred_element_type=F32)
    n_new = decay * n_st + jnp.sum(kw, axis=0, keepdims=True)

    dmat = jnp.where(before_col, b_col - b_row + ig, NEG)
    inter = b_col + m_st
    m_t = jnp.maximum(inter, jnp.max(dmat, axis=1, keepdims=True))
    a = jnp.exp(inter - m_t)
    qk = lax.dot_general(q, k, _NT, preferred_element_type=F32) * ML_SCALE
    smat = qk * jnp.exp(dmat - m_t)
    num = (a * (jnp.dot(q, c_st.astype(BF16), preferred_element_type=F32) * ML_SCALE)
           + jnp.dot(smat.astype(BF16), v, preferred_element_type=F32))
    den = (a * (jnp.sum(q.astype(F32) * n_st, axis=1, keepdims=True) * ML_SCALE)
           + jnp.sum(smat, axis=1, keepdims=True))
    h = num / jnp.maximum(jnp.abs(den), jnp.exp(-m_t))
    return h, c_new, n_new, m_new


def _mlstm_kernel(qf, kf, vf, gf, qb, kb, vb, gb, hf_ref, hb_ref, cf, nf, mf, cb, nb, mb):
    @pl.when(pl.program_id(1) == 0)
    def _():
        for r in (cf, nf, mf, cb, nb, mb):
            r[...] = jnp.zeros(r.shape, r.dtype)

    gfv = gf[0]
    h, c_new, n_new, m_new = _mlstm_chunk(qf[...], kf[...], vf[...], gfv[0:1], gfv[1:2],
                                          cf[...], nf[...], mf[...], False)
    hf_ref[...] = h
    cf[...] = c_new
    nf[...] = n_new
    mf[...] = m_new
    gbv = gb[0]
    h, c_new, n_new, m_new = _mlstm_chunk(qb[...], kb[...], vb[...], gbv[2:3], gbv[3:4],
                                          cb[...], nb[...], mb[...], True)
    hb_ref[...] = h
    cb[...] = c_new
    nb[...] = n_new
    mb[...] = m_new


def _mlstm_scan(proj, gates):
    L = ML_CHUNK
    n_lat = S // L
    assert LC == L
    ctx_blk = T_LAT // L
    qcol, kcol, vcol = 0, ML_H, (2 * ML_H * ML_DK) // ML_DV

    def fwd(bh, c):
        b = bh // ML_H
        return jnp.where(c == 0, ctx_blk + b, b * n_lat + c - 1)

    def bwd(bh, c):
        b = bh // ML_H
        return jnp.where(c == 0, ctx_blk + b, b * n_lat + n_lat - c)

    def specs(blk):
        return [pl.BlockSpec((L, ML_DK), lambda bh, c: (blk(bh, c), qcol + bh % ML_H)),
                pl.BlockSpec((L, ML_DK), lambda bh, c: (blk(bh, c), kcol + bh % ML_H)),
                pl.BlockSpec((L, ML_DV), lambda bh, c: (blk(bh, c), vcol + bh % ML_H)),
                pl.BlockSpec((1, 4, L), lambda bh, c: (bh % ML_H, 0, blk(bh, c)))]

    out_sds = jax.ShapeDtypeStruct((T_ALL, ML_H * ML_DV), F32)
    return pl.pallas_call(
        _mlstm_kernel,
        out_shape=[out_sds, out_sds],
        grid=(B * ML_H, 1 + n_lat),
        in_specs=specs(fwd) + specs(bwd),
        out_specs=[pl.BlockSpec((L, ML_DV), lambda bh, c: (fwd(bh, c), bh % ML_H)),
                   pl.BlockSpec((L, ML_DV), lambda bh, c: (bwd(bh, c), bh % ML_H))],
        scratch_shapes=[pltpu.VMEM((ML_DK, ML_DV), F32), pltpu.VMEM((1, ML_DK), F32), pltpu.VMEM((1, 1), F32),
                        pltpu.VMEM((ML_DK, ML_DV), F32), pltpu.VMEM((1, ML_DK), F32), pltpu.VMEM((1, 1), F32)],
        compiler_params=pltpu.CompilerParams(dimension_semantics=("parallel", "arbitrary")),
        name="mlstm_scan",
    )(proj, proj, proj, gates, proj, proj, proj, gates)


def _readout_kernel(hf_ref, hb_ref, o_ref, gain_ref, out_ref):
    hs = hf_ref[...] + hb_ref[...]
    o = o_ref[...].astype(F32)
    for hh in range(ML_H):
        sl = slice(hh * ML_DV, (hh + 1) * ML_DV)
        x = hs[:, sl]
        ms = jnp.mean(x * x, axis=-1, keepdims=True)
        y = x * lax.rsqrt(ms + NORM_EPS) * gain_ref[:, sl]
        og = o[:, sl]
        out_ref[:, sl] = (y * (1.0 / (1.0 + jnp.exp(-og)))).astype(out_ref.dtype)


def _mlstm_readout(hf, hb, proj, gain, n_rows):
    tm = TNORM
    wide = ML_H * ML_DV
    ocol = (2 * ML_H * ML_DK + ML_H * ML_DV) // wide
    return pl.pallas_call(
        _readout_kernel,
        out_shape=jax.ShapeDtypeStruct((n_rows, wide), BF16),
        grid=(n_rows // tm,),
        in_specs=[pl.BlockSpec((tm, wide), lambda i: (i, 0)),
                  pl.BlockSpec((tm, wide), lambda i: (i, 0)),
                  pl.BlockSpec((tm, wide), lambda i: (i, ocol)),
                  pl.BlockSpec((1, wide), lambda i: (0, 0))],
        out_specs=pl.BlockSpec((tm, wide), lambda i: (i, 0)),
        compiler_params=pltpu.CompilerParams(dimension_semantics=("parallel",)),
        name="mlstm_readout",
    )(hf, hb, proj, gain)


def _moe_kernel(be_ref, nu_ref, tok_ref, nv_ref, h_hbm, w1_ref, b1_ref, w2_ref, b2_ref, sel_ref, y_ref,
                xf_ref, xb_ref, sem):
    i = pl.program_id(0)
    f = pl.program_id(1)
    n_blocks = pl.num_programs(0)
    nf = pl.num_programs(1)
    bm = xb_ref.shape[0]
    per_tile = bm // MOE_NF
    n_used = nu_ref[0]

    def row_copy(blk, r):
        tok = tok_ref[blk * bm + r]
        return pltpu.make_async_copy(h_hbm.at[pl.ds(tok, 1)], xf_ref.at[pl.ds(r, 1)], sem)

    def wait_block(blk):
        def body(r, carry):
            row_copy(blk, r).wait()
            return carry
        lax.fori_loop(0, bm, body, 0, unroll=8)

    @pl.when(jnp.logical_and(i == 0, f == 0))
    def _():
        def body(r, carry):
            row_copy(0, r).start()
            return carry
        lax.fori_loop(0, bm, body, 0, unroll=8)

    @pl.when(jnp.logical_and(f == 0, i <= n_used))
    def _():
        wait_block(i)
        xb_ref[...] = xf_ref[...].astype(BF16)
        y_ref[...] = jnp.broadcast_to(b2_ref[0], y_ref.shape)

    def compute(m):
        nxt = jnp.minimum(i + 1, n_blocks - 1)
        for r in range(per_tile):
            row_copy(nxt, f * per_tile + r).start()
        h = jnp.dot(xb_ref[0:m, :], w1_ref[0].astype(BF16), preferred_element_type=F32) + b1_ref[0]
        glu = jnp.minimum(h, SWIGLU_LIMIT)
        glu = glu * (1.0 / (1.0 + jnp.exp(-SWIGLU_ALPHA * glu)))
        lin = jnp.clip(h, -SWIGLU_LIMIT, SWIGLU_LIMIT) + 1.0
        prod = glu * pltpu.roll(lin, h.shape[1] - 1, 1)
        act = jnp.dot(prod.astype(BF16), sel_ref[...], preferred_element_type=F32)
        y_ref[0:m, :] += jnp.dot(act.astype(BF16), w2_ref[0].astype(BF16), preferred_element_type=F32)

    n_valid = jnp.where(i < n_used, nv_ref[jnp.minimum(i, n_blocks - 1)], 0)
    for k in range(1, bm // MOE_SUB + 1):
        @pl.when(jnp.logical_and(n_valid > (k - 1) * MOE_SUB, n_valid <= k * MOE_SUB))
        def _(k=k):
            compute(k * MOE_SUB)

    @pl.when(jnp.logical_and(jnp.logical_and(i == n_blocks - 1, f == nf - 1), i < n_used))
    def _():
        wait_block(i)

    @pl.when(jnp.logical_and(i >= n_used, f == 0))
    def _():
        y_ref[...] = jnp.zeros(y_ref.shape, y_ref.dtype)


def _moe_blocks_max(n_tok):
    m = n_tok * TOP_K
    return -(-(m + N_EXP * (MOE_BM - 1)) // MOE_BM)


def _moe_experts(h, row_tok, block_expert, block_valid, n_used, layer, w1, b1, w2, b2, sel):
    bm, fh = MOE_BM, MOE_FH
    n_blocks = row_tok.shape[0] // bm
    nf = D_EXP // fh
    e0 = layer * N_EXP

    def blk(i, nu):
        return jnp.minimum(i, nu[0] - 1)

    grid_spec = pltpu.PrefetchScalarGridSpec(
        num_scalar_prefetch=4,
        grid=(n_blocks, nf),
        in_specs=[pl.BlockSpec(memory_space=pl.ANY),
                  pl.BlockSpec((1, D, 2 * fh), lambda i, f, be, nu, tok, nv: (e0 + be[blk(i, nu)], 0, f)),
                  pl.BlockSpec((1, 1, 2 * fh), lambda i, f, be, nu, tok, nv: (be[blk(i, nu)], 0, f)),
                  pl.BlockSpec((1, fh, D), lambda i, f, be, nu, tok, nv: (e0 + be[blk(i, nu)], f, 0)),
                  pl.BlockSpec((1, 1, D), lambda i, f, be, nu, tok, nv: (be[blk(i, nu)], 0, 0)),
                  pl.BlockSpec((2 * fh, fh), lambda i, f, be, nu, tok, nv: (0, 0))],
        out_specs=pl.BlockSpec((bm, D), lambda i, f, be, nu, tok, nv: (i, 0)),
        scratch_shapes=[pltpu.VMEM((bm, D), F32), pltpu.VMEM((bm, D), BF16), pltpu.SemaphoreType.DMA],
    )
    return pl.pallas_call(
        _moe_kernel,
        out_shape=jax.ShapeDtypeStruct((n_blocks * bm, D), F32),
        grid_spec=grid_spec,
        compiler_params=pltpu.CompilerParams(dimension_semantics=("arbitrary", "arbitrary")),
        name="moe_experts",
    )(block_expert, n_used, row_tok, block_valid, h, w1, b1, w2, b2, sel)


def _combine_kernel(dest_ref, x_ref, y_hbm, p_ref, gate_ref, o_ref, buf, sem):
    i = pl.program_id(0)
    tm = x_ref.shape[0]

    def row_copy(kk, r):
        d = dest_ref[(i * tm + r) * TOP_K + kk]
        return pltpu.make_async_copy(y_hbm.at[pl.ds(d, 1)], buf.at[kk, pl.ds(r, 1)], sem.at[kk])

    for kk in range(TOP_K):
        def start(r, carry, kk=kk):
            row_copy(kk, r).start()
            return carry
        lax.fori_loop(0, tm, start, 0, unroll=8)

    p = p_ref[...]
    acc = None
    for kk in range(TOP_K):
        def wait(r, carry, kk=kk):
            row_copy(kk, r).wait()
            return carry
        lax.fori_loop(0, tm, wait, 0, unroll=8)
        term = p[:, kk:kk + 1] * buf[kk]
        acc = term if acc is None else acc + term
    o_ref[...] = x_ref[...] + gate_ref[0] * acc


def _moe_combine(xs, y_rows, dest, probs, mod, gate_chunk, n_rows):
    tm = 256
    grid_spec = pltpu.PrefetchScalarGridSpec(
        num_scalar_prefetch=1,
        grid=(n_rows // tm,),
        in_specs=[pl.BlockSpec((tm, D), lambda i, dst: (i, 0)),
                  pl.BlockSpec(memory_space=pl.ANY),
                  pl.BlockSpec((tm, LANES), lambda i, dst: (i, 0)),
                  pl.BlockSpec((1, 1, D), lambda i, dst: (_mod_row(i, tm), 0, gate_chunk))],
        out_specs=pl.BlockSpec((tm, D), lambda i, dst: (i, 0)),
        scratch_shapes=[pltpu.VMEM((TOP_K, tm, D), F32), pltpu.SemaphoreType.DMA((TOP_K,))],
    )
    return pl.pallas_call(
        _combine_kernel,
        out_shape=jax.ShapeDtypeStruct((n_rows, D), F32),
        grid_spec=grid_spec,
        compiler_params=pltpu.CompilerParams(dimension_semantics=("arbitrary",)),
        name="moe_combine",
    )(dest, xs, y_rows, probs, mod)


def _moe_route(top_e, n_tok):
    m = n_tok * TOP_K
    bm = MOE_BM
    n_blocks = _moe_blocks_max(n_tok)
    e_flat = top_e.reshape(m)
    onehot = (e_flat[:, None] == jnp.arange(N_EXP, dtype=jnp.int32)[None, :]).astype(jnp.int32)
    csum = jnp.cumsum(onehot, axis=0)
    counts = csum[-1]
    rank = jnp.take_along_axis(csum, e_flat[:, None], axis=1)[:, 0] - 1
    padded = (counts + bm - 1) // bm * bm
    pad_end = jnp.cumsum(padded)
    pad_start = pad_end - padded
    dest = pad_start[e_flat] + rank
    n_used = (pad_end[-1] // bm).astype(jnp.int32).reshape(1)
    starts = jnp.arange(n_blocks, dtype=jnp.int32) * bm
    block_expert = jnp.minimum(jnp.sum((pad_end[None, :] <= starts[:, None]).astype(jnp.int32), axis=1),
                               N_EXP - 1).astype(jnp.int32)
    order = jnp.argsort(e_flat, stable=True).astype(jnp.int32)
    sort_start = jnp.cumsum(counts) - counts
    rows = jnp.arange(n_blocks * bm, dtype=jnp.int32)
    shift = jnp.repeat((sort_start - pad_start)[block_expert], bm)
    limit = jnp.repeat((pad_start + counts)[block_expert], bm)
    row_tok = jnp.where(rows < limit, order[jnp.clip(rows + shift, 0, m - 1)] // TOP_K, 0)
    block_valid = jnp.clip((pad_start + counts)[block_expert] - starts, 0, bm).astype(jnp.int32)
    return dest, row_tok, block_expert, block_valid, n_used


def _moe_layer(xs, n_rows, g, mod, router_w, router_b, layer, w1_all, b1, w2_all, b2, sel, name):
    rw = jnp.pad(router_w, ((0, 0), (0, LANES - N_EXP)))
    rb = jnp.pad(router_b, (0, LANES - N_EXP), constant_values=NEG).reshape(1, LANES)
    h, top_e, probs = _norm_mod(xs, g, mod, 3, 4, n_rows=n_rows, router_w=rw, router_b=rb, name=name + "_norm")
    dest, row_tok, block_expert, block_valid, n_used = _moe_route(top_e[:, :TOP_K], n_rows)
    y_rows = _moe_experts(h, row_tok, block_expert, block_valid, n_used, layer, w1_all,
                          b1.reshape(N_EXP, 1, 2 * D_EXP), w2_all, b2.reshape(N_EXP, 1, D), sel)
    return _moe_combine(xs, y_rows, dest, probs, mod, 5, n_rows)


def _select_matrix():
    r = np.arange(2 * MOE_FH)[:, None]
    c = np.arange(MOE_FH)[None, :]
    return jnp.asarray(r == 2 * c, BF16)


def kernel(x, c, ctx, c_ctx, ada_w, ada_b, norm_g, attn_w_in, attn_w_out, attn_qk_gain, na_rpb, mlstm_w_in,
           mlstm_gate_bias, mlstm_head_gain, mlstm_w_out, router_w, router_b, expert_w1, expert_b1,
           expert_w2, expert_b2):
    xs = jnp.concatenate([x.reshape(T_LAT, D), ctx.reshape(T_CTX, D)], axis=0)
    cc = jnp.zeros((8, D), F32).at[:B].set(c).at[MOD_CTX_ROW].set(c_ctx)
    sel = _select_matrix()

    n_layers = ada_w.shape[0]
    ada_w2 = ada_w.reshape(n_layers * D, 6 * D)
    w1_all = expert_w1.reshape(n_layers * N_EXP, D, 2 * D_EXP)
    w2_all = expert_w2.reshape(n_layers * N_EXP, D_EXP, D)

    def ada(layer):
        m = _matmul(cc, ada_w2, 6 * D, out_dtype=F32, tm=8, w_row_block=layer, silu=True,
                    bias=ada_b[layer].reshape(1, 6 * D), name="adaln")
        return m.reshape(8, 1, 6 * D)

    mod = ada(0)
    h = _norm_mod(xs, norm_g[0, 0].reshape(1, D), mod, 0, 1, n_rows=T_ALL, name="l0_norm1")
    qkv = _matmul(h, attn_w_in[0], ATTN_IN, out_dtype=BF16, name="attn_in")
    cos_rep, sin_signed = _rope_tables()
    qkv = _qk_prep(qkv, attn_qk_gain[0].reshape(4, 1, HD), cos_rep, sin_signed)
    o_buf = _na_lat(qkv, _na_bias_table(na_rpb[0]), jnp.zeros((T_ALL, D), BF16))
    o_buf = _gqa_lat(qkv, o_buf)
    o_buf = _ctx_attn(qkv, o_buf)
    xs = _matmul(o_buf, attn_w_out[0], D, out_dtype=F32, res=xs, mod=mod, gate_chunk=2, name="attn_out")
    xs = _moe_layer(xs, T_ALL, norm_g[0, 1].reshape(1, D), mod, router_w[0], router_b[0],
                    0, w1_all, expert_b1[0], w2_all, expert_b2[0], sel, "l0_moe")

    mod = ada(1)
    h = _norm_mod(xs, norm_g[1, 0].reshape(1, D), mod, 0, 1, n_rows=T_ALL, name="l1_norm1")
    proj = _matmul(h, mlstm_w_in[0], ML_MAIN, out_dtype=BF16, name="mlstm_in")
    wg = jnp.pad(mlstm_w_in[0][:, ML_MAIN:], ((0, 0), (0, LANES - 4 * ML_H)))
    bg = jnp.pad(mlstm_gate_bias[0].reshape(-1), (0, LANES - 4 * ML_H)).reshape(1, LANES)
    g = _mlstm_gates(h, wg, bg)
    gates = g[:, :4 * ML_H].T.reshape(4, ML_H, T_ALL).transpose(1, 0, 2)
    hf, hb = _mlstm_scan(proj, gates)
    hn = _mlstm_readout(hf, hb, proj, mlstm_head_gain[0].reshape(1, ML_H * ML_DV), T_LAT)
    xs = _matmul(hn, mlstm_w_out[0], D, out_dtype=F32, m_rows=T_LAT, res=xs, mod=mod, gate_chunk=2,
                 name="mlstm_out")
    out = _moe_layer(xs, T_LAT, norm_g[1, 1].reshape(1, D), mod, router_w[1], router_b[1],
                     1, w1_all, expert_b1[1], w2_all, expert_b2[1], sel, "l1_moe")
    return out.reshape(B, S, D)
```

```python
import functools

import jax
import jax.numpy as jnp
import numpy as np
from jax import lax
from jax.experimental import pallas as pl
from jax.experimental.pallas import tpu as pltpu

F32 = jnp.float32
BF16 = jnp.bfloat16

D = 2048
B = 4
S = 2048
LC = 256
GRID_W = 64
ROWS = S // GRID_W
HD = 128
NA_H = 8
GQA_H = 8
GQA_KV = 2
NA_KH = 8
NA_KW = 16
ROPE_THETA = 10000.0
ML_H = 8
ML_DK = 128
ML_DV = 256
GATE_SOFTCAP = 15.0
N_EXP = 32
TOP_K = 4
D_EXP = D
SWIGLU_ALPHA = 1.702
SWIGLU_LIMIT = 7.0
NORM_EPS = 1e-6
ATTN_IN = 3 * NA_H * HD + GQA_H * HD + 2 * GQA_KV * HD
ML_MAIN = 2 * ML_H * ML_DK + 2 * ML_H * ML_DV

T_LAT = B * S
T_CTX = B * LC
T_ALL = T_LAT + T_CTX
MOD_CTX_ROW = B

LANES = 128

TM = 1024
TN = 512
TNORM = 512
ML_CHUNK = 256
MOE_BM = 1280
MOE_FH = 256
MOE_NF = D_EXP // MOE_FH
MOE_SUB = 256
NEG = -1e30


def _mod_row(i, tm):
    n_lat = T_LAT // tm
    per_b = S // tm
    return jnp.where(i < n_lat, i // per_b, MOD_CTX_ROW)


def _mm_kernel(*refs, silu, has_bias, has_res):
    a_ref, w_ref = refs[0], refs[1]
    pos = 2
    bias_ref = res_ref = gate_ref = None
    if has_bias:
        bias_ref = refs[pos]
        pos += 1
    if has_res:
        res_ref, gate_ref = refs[pos], refs[pos + 1]
        pos += 2
    o_ref = refs[pos]
    a = a_ref[...]
    if silu:
        a = a.astype(F32)
        a = a * (1.0 / (1.0 + jnp.exp(-a)))
    acc = jnp.dot(a.astype(BF16), w_ref[...].astype(BF16), preferred_element_type=F32)
    if has_bias:
        acc = acc + bias_ref[...]
    if has_res:
        acc = res_ref[...] + gate_ref[0] * acc
    o_ref[...] = acc.astype(o_ref.dtype)


def _matmul(a, w, n_out, *, out_dtype, tm=TM, tn=TN, m_rows=None, w_row_block=0, silu=False, bias=None,
            res=None, mod=None, gate_chunk=None, name="mm"):
    m_rows = a.shape[0] if m_rows is None else m_rows
    k = a.shape[1]
    grid = (m_rows // tm, n_out // tn)
    in_specs = [pl.BlockSpec((tm, k), lambda i, j: (i, 0)),
                pl.BlockSpec((k, tn), lambda i, j: (w_row_block, j))]
    args = [a, w]
    if bias is not None:
        in_specs.append(pl.BlockSpec((1, tn), lambda i, j: (0, j)))
        args.append(bias)
    if res is not None:
        per = D // tn
        in_specs.append(pl.BlockSpec((tm, tn), lambda i, j: (i, j)))
        in_specs.append(pl.BlockSpec((1, 1, tn), lambda i, j: (_mod_row(i, tm), 0, gate_chunk * per + j)))
        args += [res, mod]
    return pl.pallas_call(
        functools.partial(_mm_kernel, silu=silu, has_bias=bias is not None, has_res=res is not None),
        out_shape=jax.ShapeDtypeStruct((m_rows, n_out), out_dtype),
        grid=grid, in_specs=in_specs,
        out_specs=pl.BlockSpec((tm, tn), lambda i, j: (i, j)),
        compiler_params=pltpu.CompilerParams(dimension_semantics=("parallel", "parallel")),
        name=name,
    )(*args)


def _norm_kernel(*refs, router):
    x_ref, g_ref, shift_ref, scale_ref = refs[:4]
    x = x_ref[...]
    ms = jnp.mean(x * x, axis=-1, keepdims=True)
    y = x * lax.rsqrt(ms + NORM_EPS) * g_ref[...]
    h = y * (1.0 + scale_ref[0]) + shift_ref[0]
    hb = h.astype(BF16)
    if not router:
        refs[4][...] = hb
        return
    rw_ref, rb_ref, h_ref, e_ref, p_ref = refs[4:]
    h_ref[...] = h
    logits = jnp.dot(hb, rw_ref[...].astype(BF16), preferred_element_type=F32) + rb_ref[...]
    lane = lax.broadcasted_iota(jnp.int32, logits.shape, 1)
    e_out = jnp.zeros(logits.shape, jnp.int32)
    v_out = jnp.full(logits.shape, NEG, F32)
    work = logits
    for kk in range(TOP_K):
        mx = jnp.max(work, axis=-1, keepdims=True)
        idx = jnp.min(jnp.where(work == mx, lane, LANES), axis=-1, keepdims=True)
        e_out = jnp.where(lane == kk, idx, e_out)
        v_out = jnp.where(lane == kk, mx, v_out)
        work = jnp.where(lane == idx, -jnp.inf, work)
    top0 = jnp.max(v_out, axis=-1, keepdims=True)
    pe = jnp.exp(v_out - top0)
    e_ref[...] = e_out
    p_ref[...] = pe / jnp.sum(pe, axis=-1, keepdims=True)


def _norm_mod(xs, g, mod, shift_chunk, scale_chunk, *, n_rows, router_w=None, router_b=None, name="norm"):
    tm = TNORM
    in_specs = [pl.BlockSpec((tm, D), lambda i: (i, 0)),
                pl.BlockSpec((1, D), lambda i: (0, 0)),
                pl.BlockSpec((1, 1, D), lambda i: (_mod_row(i, tm), 0, shift_chunk)),
                pl.BlockSpec((1, 1, D), lambda i: (_mod_row(i, tm), 0, scale_chunk))]
    args = [xs, g, mod, mod]
    router = router_w is not None
    out_shape = [jax.ShapeDtypeStruct((n_rows, D), F32 if router else BF16)]
    out_specs = [pl.BlockSpec((tm, D), lambda i: (i, 0))]
    if router:
        in_specs += [pl.BlockSpec((D, LANES), lambda i: (0, 0)),
                     pl.BlockSpec((1, LANES), lambda i: (0, 0))]
        args += [router_w, router_b]
        out_shape += [jax.ShapeDtypeStruct((n_rows, LANES), jnp.int32),
                      jax.ShapeDtypeStruct((n_rows, LANES), F32)]
        out_specs += [pl.BlockSpec((tm, LANES), lambda i: (i, 0))] * 2
    out = pl.pallas_call(
        functools.partial(_norm_kernel, router=router),
        out_shape=out_shape, grid=(n_rows // tm,), in_specs=in_specs, out_specs=out_specs,
        compiler_params=pltpu.CompilerParams(dimension_semantics=("parallel",)),
        name=name,
    )(*args)
    return out if router else out[0]


QK_PAIR = 2
N_QK_SLOTS = (2 * NA_H + GQA_H + GQA_KV) // QK_PAIR
QK_NA_SLOTS = 2 * NA_H // QK_PAIR


def _qk_col(j):
    return jnp.where(j < QK_NA_SLOTS, j, j + NA_H // QK_PAIR)


def _qk_gain_row(j):
    half = NA_H // QK_PAIR
    return jnp.where(j < half, 0, jnp.where(j < 2 * half, 1, jnp.where(j < 3 * half, 2, 3)))


def _qk_kernel(x_ref, gain_ref, cos_ref, sin_ref, o_ref, *, tm):
    i = pl.program_id(0)
    j = pl.program_id(1)
    do_rope = jnp.logical_and(j >= QK_NA_SLOTS, i < T_LAT // tm)
    for hh in range(QK_PAIR):
        sl = slice(hh * HD, (hh + 1) * HD)
        y = x_ref[:, sl].astype(F32)
        ms = jnp.mean(y * y, axis=-1, keepdims=True)
        yn = y * lax.rsqrt(ms + NORM_EPS) * gain_ref[0]
        lane = lax.broadcasted_iota(jnp.int32, yn.shape, 1)
        nxt = pltpu.roll(yn, LANES - 1, 1)
        prv = pltpu.roll(yn, 1, 1)
        partner = jnp.where(lane % 2 == 0, nxt, prv)
        yr = yn * cos_ref[...] + partner * sin_ref[...]
        o_ref[:, sl] = jnp.where(do_rope, yr, yn).astype(o_ref.dtype)


def _qk_prep(qkv, gain, cos_rep, sin_signed):
    tm = TM
    per_b = S // tm
    wide = QK_PAIR * HD
    return pl.pallas_call(
        functools.partial(_qk_kernel, tm=tm),
        out_shape=jax.ShapeDtypeStruct(qkv.shape, qkv.dtype),
        grid=(T_ALL // tm, N_QK_SLOTS),
        in_specs=[pl.BlockSpec((tm, wide), lambda i, j: (i, _qk_col(j))),
                  pl.BlockSpec((1, 1, HD), lambda i, j: (_qk_gain_row(j), 0, 0)),
                  pl.BlockSpec((tm, HD), lambda i, j: (i % per_b, 0)),
                  pl.BlockSpec((tm, HD), lambda i, j: (i % per_b, 0))],
        out_specs=pl.BlockSpec((tm, wide), lambda i, j: (i, _qk_col(j))),
        input_output_aliases={0: 0},
        compiler_params=pltpu.CompilerParams(dimension_semantics=("parallel", "parallel")),
        name="qk_prep",
    )(qkv, gain, cos_rep, sin_signed)


def _rope_tables():
    t = np.arange(S)
    row = (t // GRID_W).astype(np.float32)
    col = (t % GRID_W).astype(np.float32)
    axis_dim = HD // 2
    inv_freq = jnp.asarray(ROPE_THETA, F32) ** (-jnp.arange(0, axis_dim, 2, dtype=F32) / axis_dim)
    ang = jnp.concatenate([jnp.asarray(row)[:, None] * inv_freq, jnp.asarray(col)[:, None] * inv_freq], axis=-1)
    cos, sin = jnp.cos(ang), jnp.sin(ang)
    cos_rep = jnp.repeat(cos, 2, axis=-1)
    sin_signed = jnp.stack([-sin, sin], axis=-1).reshape(S, HD)
    return cos_rep, sin_signed


COL_NA_Q, COL_NA_K, COL_NA_V = 0, NA_H, 2 * NA_H
COL_G_Q, COL_G_K, COL_G_V = 3 * NA_H, 3 * NA_H + GQA_H, 3 * NA_H + GQA_H + GQA_KV
ATT_SCALE = HD ** -0.5
_NT = (((1,), (1,)), ((), ()))


def _attn_kernel(*refs, two):
    if two:
        q_ref, k1_ref, v1_ref, k2_ref, v2_ref, _, o_ref = refs
    else:
        q_ref, k1_ref, v1_ref, _, o_ref = refs
    q = q_ref[...]
    s1 = lax.dot_general(q, k1_ref[...], _NT, preferred_element_type=F32) * ATT_SCALE
    m = jnp.max(s1, axis=-1, keepdims=True)
    if two:
        s2 = lax.dot_general(q, k2_ref[...], _NT, preferred_element_type=F32) * ATT_SCALE
        m = jnp.maximum(m, jnp.max(s2, axis=-1, keepdims=True))
    p1 = jnp.exp(s1 - m)
    l = jnp.sum(p1, axis=-1, keepdims=True)
    o = jnp.dot(p1.astype(BF16), v1_ref[...], preferred_element_type=F32)
    if two:
        p2 = jnp.exp(s2 - m)
        l = l + jnp.sum(p2, axis=-1, keepdims=True)
        o = o + jnp.dot(p2.astype(BF16), v2_ref[...], preferred_element_type=F32)
    o_ref[...] = (o / l).astype(o_ref.dtype)


def _gqa_lat(qkv, o_buf):
    tq = 512
    nq = S // tq
    ctx_blk = T_LAT // LC
    grp = GQA_H // GQA_KV
    return pl.pallas_call(
        functools.partial(_attn_kernel, two=True),
        out_shape=jax.ShapeDtypeStruct(o_buf.shape, o_buf.dtype),
        grid=(B, GQA_H, nq),
        in_specs=[pl.BlockSpec((tq, HD), lambda b, h, t: (b * nq + t, COL_G_Q + h)),
                  pl.BlockSpec((LC, HD), lambda b, h, t: (ctx_blk + b, COL_G_K + h // grp)),
                  pl.BlockSpec((LC, HD), lambda b, h, t: (ctx_blk + b, COL_G_V + h // grp)),
                  pl.BlockSpec((S, HD), lambda b, h, t: (b, COL_G_K + h // grp)),
                  pl.BlockSpec((S, HD), lambda b, h, t: (b, COL_G_V + h // grp)),
                  pl.BlockSpec(memory_space=pl.ANY)],
        out_specs=pl.BlockSpec((tq, HD), lambda b, h, t: (b * nq + t, NA_H + h)),
        input_output_aliases={5: 0},
        compiler_params=pltpu.CompilerParams(dimension_semantics=("parallel", "parallel", "parallel")),
        name="gqa_lat",
    )(qkv, qkv, qkv, qkv, qkv, o_buf)


def _ctx_attn(qkv, o_buf):
    ctx_blk = T_LAT // LC
    grp = GQA_H // GQA_KV

    def qcol(h):
        return jnp.where(h < NA_H, COL_NA_Q + h, COL_G_Q + h - NA_H)

    def kcol(h):
        return jnp.where(h < NA_H, COL_NA_K + h, COL_G_K + (h - NA_H) // grp)

    def vcol(h):
        return jnp.where(h < NA_H, COL_NA_V + h, COL_G_V + (h - NA_H) // grp)

    return pl.pallas_call(
        functools.partial(_attn_kernel, two=False),
        out_shape=jax.ShapeDtypeStruct(o_buf.shape, o_buf.dtype),
        grid=(B, NA_H + GQA_H),
        in_specs=[pl.BlockSpec((LC, HD), lambda b, h: (ctx_blk + b, qcol(h))),
                  pl.BlockSpec((LC, HD), lambda b, h: (ctx_blk + b, kcol(h))),
                  pl.BlockSpec((LC, HD), lambda b, h: (ctx_blk + b, vcol(h))),
                  pl.BlockSpec(memory_space=pl.ANY)],
        out_specs=pl.BlockSpec((LC, HD), lambda b, h: (ctx_blk + b, h)),
        input_output_aliases={3: 0},
        compiler_params=pltpu.CompilerParams(dimension_semantics=("parallel", "parallel")),
        name="ctx_attn",
    )(qkv, qkv, qkv, o_buf)


def _na_kernel(q_ref, k_ref, v_ref, kc_ref, vc_ref, bias_ref, _, o_ref):
    kc = kc_ref[...]
    vc = vc_ref[...]
    n_win = NA_KH * GRID_W

    def body(r, carry):
        r0 = jnp.clip(r - NA_KH // 2, 0, ROWS - NA_KH)
        rel0 = r0 - r + NA_KH - 1
        q = q_ref[pl.ds(pl.multiple_of(r * GRID_W, GRID_W), GRID_W), :]
        kw = k_ref[pl.ds(pl.multiple_of(r0 * GRID_W, GRID_W), n_win), :]
        vw = v_ref[pl.ds(pl.multiple_of(r0 * GRID_W, GRID_W), n_win), :]
        sw = lax.dot_general(q, kw, _NT, preferred_element_type=F32) * ATT_SCALE + bias_ref[0, rel0]
        sc = lax.dot_general(q, kc, _NT, preferred_element_type=F32) * ATT_SCALE
        m = jnp.maximum(jnp.max(sw, axis=-1, keepdims=True), jnp.max(sc, axis=-1, keepdims=True))
        pw = jnp.exp(sw - m)
        pc = jnp.exp(sc - m)
        l = jnp.sum(pw, axis=-1, keepdims=True) + jnp.sum(pc, axis=-1, keepdims=True)
        o = (jnp.dot(pw.astype(BF16), vw, preferred_element_type=F32)
             + jnp.dot(pc.astype(BF16), vc, preferred_element_type=F32))
        o_ref[pl.ds(pl.multiple_of(r * GRID_W, GRID_W), GRID_W), :] = (o / l).astype(o_ref.dtype)
        return carry

    lax.fori_loop(0, ROWS, body, 0, unroll=4)


def _na_bias_table(rpb):
    n_rel = 2 * NA_KW - 1
    qc = np.arange(GRID_W)[:, None]
    kc = np.arange(GRID_W)[None, :]
    start = np.clip(qc - NA_KW // 2, 0, GRID_W - NA_KW)
    valid = (kc >= start) & (kc < start + NA_KW)
    onehot = ((np.arange(n_rel)[:, None, None] == (kc - qc + NA_KW - 1)[None]) & valid[None]).astype(np.float32)
    exp = jnp.einsum("hrc,cqk->hqrk", rpb, jnp.asarray(onehot), precision=lax.Precision.HIGHEST)
    exp = exp + jnp.asarray(np.where(valid, 0.0, NEG).astype(np.float32))[None, :, None, :]
    tabs = [exp[:, :, rel0:rel0 + NA_KH, :].reshape(NA_H, GRID_W, NA_KH * GRID_W) for rel0 in range(NA_KH)]
    return jnp.stack(tabs, axis=1)


def _na_lat(qkv, bias_tab, o_buf):
    ctx_blk = T_LAT // LC
    return pl.pallas_call(
        _na_kernel,
        out_shape=jax.ShapeDtypeStruct(o_buf.shape, o_buf.dtype),
        grid=(B, NA_H),
        in_specs=[pl.BlockSpec((S, HD), lambda b, h: (b, COL_NA_Q + h)),
                  pl.BlockSpec((S, HD), lambda b, h: (b, COL_NA_K + h)),
                  pl.BlockSpec((S, HD), lambda b, h: (b, COL_NA_V + h)),
                  pl.BlockSpec((LC, HD), lambda b, h: (ctx_blk + b, COL_NA_K + h)),
                  pl.BlockSpec((LC, HD), lambda b, h: (ctx_blk + b, COL_NA_V + h)),
                  pl.BlockSpec((1, NA_KH, GRID_W, NA_KH * GRID_W), lambda b, h: (h, 0, 0, 0)),
                  pl.BlockSpec(memory_space=pl.ANY)],
        out_specs=pl.BlockSpec((S, HD), lambda b, h: (b, h)),
        input_output_aliases={6: 0},
        compiler_params=pltpu.CompilerParams(dimension_semantics=("parallel", "parallel")),
        name="na_lat",
    )(qkv, qkv, qkv, qkv, qkv, bias_tab, o_buf)


def _gate_kernel(a_ref, w_ref, b_ref, o_ref):
    g = jnp.dot(a_ref[...], w_ref[...].astype(BF16), preferred_element_type=F32) + b_ref[...]
    g = GATE_SOFTCAP * jnp.tanh(g / GATE_SOFTCAP)
    lane = lax.broadcasted_iota(jnp.int32, g.shape, 1)
    is_forget = (lane // ML_H) % 2 == 1
    log_sig = jnp.minimum(g, 0.0) - jnp.log(1.0 + jnp.exp(-jnp.abs(g)))
    o_ref[...] = jnp.where(is_forget, log_sig, g)


def _mlstm_gates(h, wg, bg):
    tm = TM
    return pl.pallas_call(
        _gate_kernel,
        out_shape=jax.ShapeDtypeStruct((T_ALL, LANES), F32),
        grid=(T_ALL // tm,),
        in_specs=[pl.BlockSpec((tm, D), lambda i: (i, 0)),
                  pl.BlockSpec((D, LANES), lambda i: (0, 0)),
                  pl.BlockSpec((1, LANES), lambda i: (0, 0))],
        out_specs=pl.BlockSpec((tm, LANES), lambda i: (i, 0)),
        compiler_params=pltpu.CompilerParams(dimension_semantics=("parallel",)),
        name="mlstm_gates",
    )(h, wg, bg)


ML_SCALE = ML_DK ** -0.5
_TN = (((0,), (0,)), ((), ()))


def _mlstm_chunk(q, k, v, ig, fg, c_st, n_st, m_st, rev):
    L = q.shape[0]
    ti = lax.broadcasted_iota(jnp.int32, (L, L), 0)
    si = lax.broadcasted_iota(jnp.int32, (L, L), 1)
    eye = ti == si
    before_col = (si >= ti) if rev else (si <= ti)
    before_row = (ti >= si) if rev else (ti <= si)
    f_col = jnp.sum(jnp.where(eye, fg, 0.0), axis=1, keepdims=True)
    i_col = jnp.sum(jnp.where(eye, ig, 0.0), axis=1, keepdims=True)
    b_col = jnp.sum(jnp.where(before_col, fg, 0.0), axis=1, keepdims=True)
    b_row = jnp.sum(jnp.where(before_row, f_col, 0.0), axis=0, keepdims=True)
    total = jnp.sum(fg, axis=1, keepdims=True)
    g_col = total - b_col + i_col
    m_new = jnp.maximum(total + m_st, jnp.max(g_col, axis=0, keepdims=True))
    decay = jnp.exp(total + m_st - m_new)
    wk = jnp.exp(g_col - m_new)
    kw = k.astype(F32) * wk
    c_new = decay * c_st + lax.dot_general(kw.astype(BF16), v, _TN, preferred_element_type=F32)
    n_new = decay * n_st + jnp.sum(kw, axis=0, keepdims=True)

    dmat = jnp.where(before_col, b_col - b_row + ig, NEG)
    inter = b_col + m_st
    m_t = jnp.maximum(inter, jnp.max(dmat, axis=1, keepdims=True))
    a = jnp.exp(inter - m_t)
    qk = lax.dot_general(q, k, _NT, preferred_element_type=F32) * ML_SCALE
    smat = qk * jnp.exp(dmat - m_t)
    num = (a * (jnp.dot(q, c_st.astype(BF16), preferred_element_type=F32) * ML_SCALE)
           + jnp.dot(smat.astype(BF16), v, preferred_element_type=F32))
    den = (a * (jnp.sum(q.astype(F32) * n_st, axis=1, keepdims=True) * ML_SCALE)
           + jnp.sum(smat, axis=1, keepdims=True))
    h = num / jnp.maximum(jnp.abs(den), jnp.exp(-m_t))
    return h, c_new, n_new, m_new


def _mlstm_kernel(qf, kf, vf, gf, qb, kb, vb, gb, hf_ref, hb_ref, cf, nf, mf, cb, nb, mb):
    @pl.when(pl.program_id(1) == 0)
    def _():
        for r in (cf, nf, mf, cb, nb, mb):
            r[...] = jnp.zeros(r.shape, r.dtype)

    gfv = gf[0]
    h, c_new, n_new, m_new = _mlstm_chunk(qf[...], kf[...], vf[...], gfv[0:1], gfv[1:2],
                                          cf[...], nf[...], mf[...], False)
    hf_ref[...] = h
    cf[...] = c_new
    nf[...] = n_new
    mf[...] = m_new
    gbv = gb[0]
    h, c_new, n_new, m_new = _mlstm_chunk(qb[...], kb[...], vb[...], gbv[2:3], gbv[3:4],
                                          cb[...], nb[...], mb[...], True)
    hb_ref[...] = h
    cb[...] = c_new
    nb[...] = n_new
    mb[...] = m_new


def _mlstm_scan(proj, gates):
    L = ML_CHUNK
    n_lat = S // L
    assert LC == L
    ctx_blk = T_LAT // L
    qcol, kcol, vcol = 0, ML_H, (2 * ML_H * ML_DK) // ML_DV

    def fwd(bh, c):
        b = bh // ML_H
        return jnp.where(c == 0, ctx_blk + b, b * n_lat + c - 1)

    def bwd(bh, c):
        b = bh // ML_H
        return jnp.where(c == 0, ctx_blk + b, b * n_lat + n_lat - c)

    def specs(blk):
        return [pl.BlockSpec((L, ML_DK), lambda bh, c: (blk(bh, c), qcol + bh % ML_H)),
                pl.BlockSpec((L, ML_DK), lambda bh, c: (blk(bh, c), kcol + bh % ML_H)),
                pl.BlockSpec((L, ML_DV), lambda bh, c: (blk(bh, c), vcol + bh % ML_H)),
                pl.BlockSpec((1, 4, L), lambda bh, c: (bh % ML_H, 0, blk(bh, c)))]

    out_sds = jax.ShapeDtypeStruct((T_ALL, ML_H * ML_DV), F32)
    return pl.pallas_call(
        _mlstm_kernel,
        out_shape=[out_sds, out_sds],
        grid=(B * ML_H, 1 + n_lat),
        in_specs=specs(fwd) + specs(bwd),
        out_specs=[pl.BlockSpec((L, ML_DV), lambda bh, c: (fwd(bh, c), bh % ML_H)),
                   pl.BlockSpec((L, ML_DV), lambda bh, c: (bwd(bh, c), bh % ML_H))],
        scratch_shapes=[pltpu.VMEM((ML_DK, ML_DV), F32), pltpu.VMEM((1, ML_DK), F32), pltpu.VMEM((1, 1), F32),
                        pltpu.VMEM((ML_DK, ML_DV), F32), pltpu.VMEM((1, ML_DK), F32), pltpu.VMEM((1, 1), F32)],
        compiler_params=pltpu.CompilerParams(dimension_semantics=("parallel", "arbitrary")),
        name="mlstm_scan",
    )(proj, proj, proj, gates, proj, proj, proj, gates)


def _readout_kernel(hf_ref, hb_ref, o_ref, gain_ref, out_ref):
    hs = hf_ref[...] + hb_ref[...]
    o = o_ref[...].astype(F32)
    for hh in range(ML_H):
        sl = slice(hh * ML_DV, (hh + 1) * ML_DV)
        x = hs[:, sl]
        ms = jnp.mean(x * x, axis=-1, keepdims=True)
        y = x * lax.rsqrt(ms + NORM_EPS) * gain_ref[:, sl]
        og = o[:, sl]
        out_ref[:, sl] = (y * (1.0 / (1.0 + jnp.exp(-og)))).astype(out_ref.dtype)


def _mlstm_readout(hf, hb, proj, gain, n_rows):
    tm = TNORM
    wide = ML_H * ML_DV
    ocol = (2 * ML_H * ML_DK + ML_H * ML_DV) // wide
    return pl.pallas_call(
        _readout_kernel,
        out_shape=jax.ShapeDtypeStruct((n_rows, wide), BF16),
        grid=(n_rows // tm,),
        in_specs=[pl.BlockSpec((tm, wide), lambda i: (i, 0)),
                  pl.BlockSpec((tm, wide), lambda i: (i, 0)),
                  pl.BlockSpec((tm, wide), lambda i: (i, ocol)),
                  pl.BlockSpec((1, wide), lambda i: (0, 0))],
        out_specs=pl.BlockSpec((tm, wide), lambda i: (i, 0)),
        compiler_params=pltpu.CompilerParams(dimension_semantics=("parallel",)),
        name="mlstm_readout",
    )(hf, hb, proj, gain)


def _moe_kernel(be_ref, nu_ref, tok_ref, nv_ref, h_hbm, w1_ref, b1_ref, w2_ref, b2_ref, sel_ref, y_ref,
                xf_ref, xb_ref, sem):
    i = pl.program_id(0)
    f = pl.program_id(1)
    n_blocks = pl.num_programs(0)
    nf = pl.num_programs(1)
    bm = xb_ref.shape[0]
    per_tile = bm // MOE_NF
    n_used = nu_ref[0]

    def row_copy(blk, r):
        tok = tok_ref[blk * bm + r]
        return pltpu.make_async_copy(h_hbm.at[pl.ds(tok, 1)], xf_ref.at[pl.ds(r, 1)], sem)

    def wait_block(blk):
        def body(r, carry):
            row_copy(blk, r).wait()
            return carry
        lax.fori_loop(0, bm, body, 0, unroll=8)

    @pl.when(jnp.logical_and(i == 0, f == 0))
    def _():
        def body(r, carry):
            row_copy(0, r).start()
            return carry
        lax.fori_loop(0, bm, body, 0, unroll=8)

    @pl.when(jnp.logical_and(f == 0, i <= n_used))
    def _():
        wait_block(i)
        xb_ref[...] = xf_ref[...].astype(BF16)
        y_ref[...] = jnp.broadcast_to(b2_ref[0], y_ref.shape)

    def compute(m):
        nxt = jnp.minimum(i + 1, n_blocks - 1)
        for r in range(per_tile):
            row_copy(nxt, f * per_tile + r).start()
        h = jnp.dot(xb_ref[0:m, :], w1_ref[0].astype(BF16), preferred_element_type=F32) + b1_ref[0]
        glu = jnp.minimum(h, SWIGLU_LIMIT)
        glu = glu * (1.0 / (1.0 + jnp.exp(-SWIGLU_ALPHA * glu)))
        lin = jnp.clip(h, -SWIGLU_LIMIT, SWIGLU_LIMIT) + 1.0
        prod = glu * pltpu.roll(lin, h.shape[1] - 1, 1)
        act = jnp.dot(prod.astype(BF16), sel_ref[...], preferred_element_type=F32)
        y_ref[0:m, :] += jnp.dot(act.astype(BF16), w2_ref[0].astype(BF16), preferred_element_type=F32)

    n_valid = jnp.where(i < n_used, nv_ref[jnp.minimum(i, n_blocks - 1)], 0)
    for k in range(1, bm // MOE_SUB + 1):
        @pl.when(jnp.logical_and(n_valid > (k - 1) * MOE_SUB, n_valid <= k * MOE_SUB))
        def _(k=k):
            compute(k * MOE_SUB)

    @pl.when(jnp.logical_and(jnp.logical_and(i == n_blocks - 1, f == nf - 1), i < n_used))
    def _():
        wait_block(i)

    @pl.when(jnp.logical_and(i >= n_used, f == 0))
    def _():
        y_ref[...] = jnp.zeros(y_ref.shape, y_ref.dtype)


def _moe_blocks_max(n_tok):
    m = n_tok * TOP_K
    return -(-(m + N_EXP * (MOE_BM - 1)) // MOE_BM)


def _moe_experts(h, row_tok, block_expert, block_valid, n_used, layer, w1, b1, w2, b2, sel):
    bm, fh = MOE_BM, MOE_FH
    n_blocks = row_tok.shape[0] // bm
    nf = D_EXP // fh
    e0 = layer * N_EXP

    def blk(i, nu):
        return jnp.minimum(i, nu[0] - 1)

    grid_spec = pltpu.PrefetchScalarGridSpec(
        num_scalar_prefetch=4,
        grid=(n_blocks, nf),
        in_specs=[pl.BlockSpec(memory_space=pl.ANY),
                  pl.BlockSpec((1, D, 2 * fh), lambda i, f, be, nu, tok, nv: (e0 + be[blk(i, nu)], 0, f)),
                  pl.BlockSpec((1, 1, 2 * fh), lambda i, f, be, nu, tok, nv: (be[blk(i, nu)], 0, f)),
                  pl.BlockSpec((1, fh, D), lambda i, f, be, nu, tok, nv: (e0 + be[blk(i, nu)], f, 0)),
                  pl.BlockSpec((1, 1, D), lambda i, f, be, nu, tok, nv: (be[blk(i, nu)], 0, 0)),
                  pl.BlockSpec((2 * fh, fh), lambda i, f, be, nu, tok, nv: (0, 0))],
        out_specs=pl.BlockSpec((bm, D), lambda i, f, be, nu, tok, nv: (i, 0)),
        scratch_shapes=[pltpu.VMEM((bm, D), F32), pltpu.VMEM((bm, D), BF16), pltpu.SemaphoreType.DMA],
    )
    return pl.pallas_call(
        _moe_kernel,
        out_shape=jax.ShapeDtypeStruct((n_blocks * bm, D), F32),
        grid_spec=grid_spec,
        compiler_params=pltpu.CompilerParams(dimension_semantics=("arbitrary", "arbitrary")),
        name="moe_experts",
    )(block_expert, n_used, row_tok, block_valid, h, w1, b1, w2, b2, sel)


def _combine_kernel(dest_ref, x_ref, y_hbm, p_ref, gate_ref, o_ref, buf, sem):
    i = pl.program_id(0)
    tm = x_ref.shape[0]

    def row_copy(kk, r):
        d = dest_ref[(i * tm + r) * TOP_K + kk]
        return pltpu.make_async_copy(y_hbm.at[pl.ds(d, 1)], buf.at[kk, pl.ds(r, 1)], sem.at[kk])

    for kk in range(TOP_K):
        def start(r, carry, kk=kk):
            row_copy(kk, r).start()
            return carry
        lax.fori_loop(0, tm, start, 0, unroll=8)

    p = p_ref[...]
    acc = None
    for kk in range(TOP_K):
        def wait(r, carry, kk=kk):
            row_copy(kk, r).wait()
            return carry
        lax.fori_loop(0, tm, wait, 0, unroll=8)
        term = p[:, kk:kk + 1] * buf[kk]
        acc = term if acc is None else acc + term
    o_ref[...] = x_ref[...] + gate_ref[0] * acc


def _moe_combine(xs, y_rows, dest, probs, mod, gate_chunk, n_rows):
    tm = 256
    grid_spec = pltpu.PrefetchScalarGridSpec(
        num_scalar_prefetch=1,
        grid=(n_rows // tm,),
        in_specs=[pl.BlockSpec((tm, D), lambda i, dst: (i, 0)),
                  pl.BlockSpec(memory_space=pl.ANY),
                  pl.BlockSpec((tm, LANES), lambda i, dst: (i, 0)),
                  pl.BlockSpec((1, 1, D), lambda i, dst: (_mod_row(i, tm), 0, gate_chunk))],
        out_specs=pl.BlockSpec((tm, D), lambda i, dst: (i, 0)),
        scratch_shapes=[pltpu.VMEM((TOP_K, tm, D), F32), pltpu.SemaphoreType.DMA((TOP_K,))],
    )
    return pl.pallas_call(
        _combine_kernel,
        out_shape=jax.ShapeDtypeStruct((n_rows, D), F32),
        grid_spec=grid_spec,
        compiler_params=pltpu.CompilerParams(dimension_semantics=("arbitrary",)),
        name="moe_combine",
    )(dest, xs, y_rows, probs, mod)


def _moe_route(top_e, n_tok):
    m = n_tok * TOP_K
    bm = MOE_BM
    n_blocks = _moe_blocks_max(n_tok)
    e_flat = top_e.reshape(m)
    onehot = (e_flat[:, None] == jnp.arange(N_EXP, dtype=jnp.int32)[None, :]).astype(jnp.int32)
    csum = jnp.cumsum(onehot, axis=0)
    counts = csum[-1]
    rank = jnp.take_along_axis(csum, e_flat[:, None], axis=1)[:, 0] - 1
    padded = (counts + bm - 1) // bm * bm
    pad_end = jnp.cumsum(padded)
    pad_start = pad_end - padded
    dest = pad_start[e_flat] + rank
    n_used = (pad_end[-1] // bm).astype(jnp.int32).reshape(1)
    starts = jnp.arange(n_blocks, dtype=jnp.int32) * bm
    block_expert = jnp.minimum(jnp.sum((pad_end[None, :] <= starts[:, None]).astype(jnp.int32), axis=1),
                               N_EXP - 1).astype(jnp.int32)
    order = jnp.argsort(e_flat, stable=True).astype(jnp.int32)
    sort_start = jnp.cumsum(counts) - counts
    rows = jnp.arange(n_blocks * bm, dtype=jnp.int32)
    shift = jnp.repeat((sort_start - pad_start)[block_expert], bm)
    limit = jnp.repeat((pad_start + counts)[block_expert], bm)
    row_tok = jnp.where(rows < limit, order[jnp.clip(rows + shift, 0, m - 1)] // TOP_K, 0)
    block_valid = jnp.clip((pad_start + counts)[block_expert] - starts, 0, bm).astype(jnp.int32)
    return dest, row_tok, block_expert, block_valid, n_used


def _moe_layer(xs, n_rows, g, mod, router_w, router_b, layer, w1_all, b1, w2_all, b2, sel, name):
    rw = jnp.pad(router_w, ((0, 0), (0, LANES - N_EXP)))
    rb = jnp.pad(router_b, (0, LANES - N_EXP), constant_values=NEG).reshape(1, LANES)
    h, top_e, probs = _norm_mod(xs, g, mod, 3, 4, n_rows=n_rows, router_w=rw, router_b=rb, name=name + "_norm")
    dest, row_tok, block_expert, block_valid, n_used = _moe_route(top_e[:, :TOP_K], n_rows)
    y_rows = _moe_experts(h, row_tok, block_expert, block_valid, n_used, layer, w1_all,
                          b1.reshape(N_EXP, 1, 2 * D_EXP), w2_all, b2.reshape(N_EXP, 1, D), sel)
    return _moe_combine(xs, y_rows, dest, probs, mod, 5, n_rows)


def _select_matrix():
    r = np.arange(2 * MOE_FH)[:, None]
    c = np.arange(MOE_FH)[None, :]
    return jnp.asarray(r == 2 * c, BF16)


def kernel(x, c, ctx, c_ctx, ada_w, ada_b, norm_g, attn_w_in, attn_w_out, attn_qk_gain, na_rpb, mlstm_w_in,
           mlstm_gate_bias, mlstm_head_gain, mlstm_w_out, router_w, router_b, expert_w1, expert_b1,
           expert_w2, expert_b2):
    xs = jnp.concatenate([x.reshape(T_LAT, D), ctx.reshape(T_CTX, D)], axis=0)
    cc = jnp.zeros((8, D), F32).at[:B].set(c).at[MOD_CTX_ROW].set(c_ctx)
    sel = _select_matrix()

    n_layers = ada_w.shape[0]
    ada_w2 = ada_w.reshape(n_layers * D, 6 * D)
    w1_all = expert_w1.reshape(n_layers * N_EXP, D, 2 * D_EXP)
    w2_all = expert_w2.reshape(n_layers * N_EXP, D_EXP, D)

    def ada(layer):
        m = _matmul(cc, ada_w2, 6 * D, out_dtype=F32, tm=8, w_row_block=layer, silu=True,
                    bias=ada_b[layer].reshape(1, 6 * D), name="adaln")
        return m.reshape(8, 1, 6 * D)

    mod = ada(0)
    h = _norm_mod(xs, norm_g[0, 0].reshape(1, D), mod, 0, 1, n_rows=T_ALL, name="l0_norm1")
    qkv = _matmul(h, attn_w_in[0], ATTN_IN, out_dtype=BF16, name="attn_in")
    cos_rep, sin_signed = _rope_tables()
    qkv = _qk_prep(qkv, attn_qk_gain[0].reshape(4, 1, HD), cos_rep, sin_signed)
    o_buf = _na_lat(qkv, _na_bias_table(na_rpb[0]), jnp.zeros((T_ALL, D), BF16))
    o_buf = _gqa_lat(qkv, o_buf)
    o_buf = _ctx_attn(qkv, o_buf)
    xs = _matmul(o_buf, attn_w_out[0], D, out_dtype=F32, res=xs, mod=mod, gate_chunk=2, name="attn_out")
    xs = _moe_layer(xs, T_ALL, norm_g[0, 1].reshape(1, D), mod, router_w[0], router_b[0],
                    0, w1_all, expert_b1[0], w2_all, expert_b2[0], sel, "l0_moe")

    mod = ada(1)
    h = _norm_mod(xs, norm_g[1, 0].reshape(1, D), mod, 0, 1, n_rows=T_ALL, name="l1_norm1")
    proj = _matmul(h, mlstm_w_in[0], ML_MAIN, out_dtype=BF16, name="mlstm_in")
    wg = jnp.pad(mlstm_w_in[0][:, ML_MAIN:], ((0, 0), (0, LANES - 4 * ML_H)))
    bg = jnp.pad(mlstm_gate_bias[0].reshape(-1), (0, LANES - 4 * ML_H)).reshape(1, LANES)
    g = _mlstm_gates(h, wg, bg)
    gates = g[:, :4 * ML_H].T.reshape(4, ML_H, T_ALL).transpose(1, 0, 2)
    hf, hb = _mlstm_scan(proj, gates)
    hn = _mlstm_readout(hf, hb, proj, mlstm_head_gain[0].reshape(1, ML_H * ML_DV), T_LAT)
    xs = _matmul(hn, mlstm_w_out[0], D, out_dtype=F32, m_rows=T_LAT, res=xs, mod=mod, gate_chunk=2,
                 name="mlstm_out")
    out = _moe_layer(xs, T_LAT, norm_g[1, 1].reshape(1, D), mod, router_w[1], router_b[1],
                     1, w1_all, expert_b1[1], w2_all, expert_b2[1], sel, "l1_moe")
    return out.reshape(B, S, D)
```

```python
import functools

import jax
import jax.numpy as jnp
import numpy as np
from jax import lax
from jax.experimental import pallas as pl
from jax.experimental.pallas import tpu as pltpu

F32 = jnp.float32
BF16 = jnp.bfloat16

D = 2048
B = 4
S = 2048
LC = 256
GRID_W = 64
ROWS = S // GRID_W
HD = 128
NA_H = 8
GQA_H = 8
GQA_KV = 2
NA_KH = 8
NA_KW = 16
ROPE_THETA = 10000.0
ML_H = 8
ML_DK = 128
ML_DV = 256
GATE_SOFTCAP = 15.0
N_EXP = 32
TOP_K = 4
D_EXP = D
SWIGLU_ALPHA = 1.702
SWIGLU_LIMIT = 7.0
NORM_EPS = 1e-6
ATTN_IN = 3 * NA_H * HD + GQA_H * HD + 2 * GQA_KV * HD
ML_MAIN = 2 * ML_H * ML_DK + 2 * ML_H * ML_DV

T_LAT = B * S
T_CTX = B * LC
T_ALL = T_LAT + T_CTX
MOD_CTX_ROW = B

LANES = 128

TM = 1024
TN = 512
TNORM = 512
ML_CHUNK = 256
MOE_BM = 768
MOE_FH = 512
MOE_NF = D_EXP // MOE_FH
MOE_SUB = 256
NEG = -1e30


def _mod_row(i, tm):
    n_lat = T_LAT // tm
    per_b = S // tm
    return jnp.where(i < n_lat, i // per_b, MOD_CTX_ROW)


def _mm_kernel(*refs, silu, has_bias, has_res):
    a_ref, w_ref = refs[0], refs[1]
    pos = 2
    bias_ref = res_ref = gate_ref = None
    if has_bias:
        bias_ref = refs[pos]
        pos += 1
    if has_res:
        res_ref, gate_ref = refs[pos], refs[pos + 1]
        pos += 2
    o_ref = refs[pos]
    a = a_ref[...]
    if silu:
        a = a.astype(F32)
        a = a * (1.0 / (1.0 + jnp.exp(-a)))
    acc = jnp.dot(a.astype(BF16), w_ref[...].astype(BF16), preferred_element_type=F32)
    if has_bias:
        acc = acc + bias_ref[...]
    if has_res:
        acc = res_ref[...] + gate_ref[0] * acc
    o_ref[...] = acc.astype(o_ref.dtype)


def _matmul(a, w, n_out, *, out_dtype, tm=TM, tn=TN, m_rows=None, w_row_block=0, silu=False, bias=None,
            res=None, mod=None, gate_chunk=None, name="mm"):
    m_rows = a.shape[0] if m_rows is None else m_rows
    k = a.shape[1]
    grid = (m_rows // tm, n_out // tn)
    in_specs = [pl.BlockSpec((tm, k), lambda i, j: (i, 0)),
                pl.BlockSpec((k, tn), lambda i, j: (w_row_block, j))]
    args = [a, w]
    if bias is not None:
        in_specs.append(pl.BlockSpec((1, tn), lambda i, j: (0, j)))
        args.append(bias)
    if res is not None:
        per = D // tn
        in_specs.append(pl.BlockSpec((tm, tn), lambda i, j: (i, j)))
        in_specs.append(pl.BlockSpec((1, 1, tn), lambda i, j: (_mod_row(i, tm), 0, gate_chunk * per + j)))
        args += [res, mod]
    return pl.pallas_call(
        functools.partial(_mm_kernel, silu=silu, has_bias=bias is not None, has_res=res is not None),
        out_shape=jax.ShapeDtypeStruct((m_rows, n_out), out_dtype),
        grid=grid, in_specs=in_specs,
        out_specs=pl.BlockSpec((tm, tn), lambda i, j: (i, j)),
        compiler_params=pltpu.CompilerParams(dimension_semantics=("parallel", "parallel")),
        name=name,
    )(*args)


def _norm_kernel(*refs, router):
    x_ref, g_ref, shift_ref, scale_ref = refs[:4]
    x = x_ref[...]
    ms = jnp.mean(x * x, axis=-1, keepdims=True)
    y = x * lax.rsqrt(ms + NORM_EPS) * g_ref[...]
    h = y * (1.0 + scale_ref[0]) + shift_ref[0]
    hb = h.astype(BF16)
    if not router:
        refs[4][...] = hb
        return
    rw_ref, rb_ref, h_ref, e_ref, p_ref = refs[4:]
    h_ref[...] = h
    logits = jnp.dot(hb, rw_ref[...].astype(BF16), preferred_element_type=F32) + rb_ref[...]
    lane = lax.broadcasted_iota(jnp.int32, logits.shape, 1)
    e_out = jnp.zeros(logits.shape, jnp.int32)
    v_out = jnp.full(logits.shape, NEG, F32)
    work = logits
    for kk in range(TOP_K):
        mx = jnp.max(work, axis=-1, keepdims=True)
        idx = jnp.min(jnp.where(work == mx, lane, LANES), axis=-1, keepdims=True)
        e_out = jnp.where(lane == kk, idx, e_out)
        v_out = jnp.where(lane == kk, mx, v_out)
        work = jnp.where(lane == idx, -jnp.inf, work)
    top0 = jnp.max(v_out, axis=-1, keepdims=True)
    pe = jnp.exp(v_out - top0)
    e_ref[...] = e_out
    p_ref[...] = pe / jnp.sum(pe, axis=-1, keepdims=True)


def _norm_mod(xs, g, mod, shift_chunk, scale_chunk, *, n_rows, router_w=None, router_b=None, name="norm"):
    tm = TNORM
    in_specs = [pl.BlockSpec((tm, D), lambda i: (i, 0)),
                pl.BlockSpec((1, D), lambda i: (0, 0)),
                pl.BlockSpec((1, 1, D), lambda i: (_mod_row(i, tm), 0, shift_chunk)),
                pl.BlockSpec((1, 1, D), lambda i: (_mod_row(i, tm), 0, scale_chunk))]
    args = [xs, g, mod, mod]
    router = router_w is not None
    out_shape = [jax.ShapeDtypeStruct((n_rows, D), F32 if router else BF16)]
    out_specs = [pl.BlockSpec((tm, D), lambda i: (i, 0))]
    if router:
        in_specs += [pl.BlockSpec((D, LANES), lambda i: (0, 0)),
                     pl.BlockSpec((1, LANES), lambda i: (0, 0))]
        args += [router_w, router_b]
        out_shape += [jax.ShapeDtypeStruct((n_rows, LANES), jnp.int32),
                      jax.ShapeDtypeStruct((n_rows, LANES), F32)]
        out_specs += [pl.BlockSpec((tm, LANES), lambda i: (i, 0))] * 2
    out = pl.pallas_call(
        functools.partial(_norm_kernel, router=router),
        out_shape=out_shape, grid=(n_rows // tm,), in_specs=in_specs, out_specs=out_specs,
        compiler_params=pltpu.CompilerParams(dimension_semantics=("parallel",)),
        name=name,
    )(*args)
    return out if router else out[0]


QK_PAIR = 2
N_QK_SLOTS = (2 * NA_H + GQA_H + GQA_KV) // QK_PAIR
QK_NA_SLOTS = 2 * NA_H // QK_PAIR


def _qk_col(j):
    return jnp.where(j < QK_NA_SLOTS, j, j + NA_H // QK_PAIR)


def _qk_gain_row(j):
    half = NA_H // QK_PAIR
    return jnp.where(j < half, 0, jnp.where(j < 2 * half, 1, jnp.where(j < 3 * half, 2, 3)))


def _qk_kernel(x_ref, gain_ref, cos_ref, sin_ref, o_ref, *, tm):
    i = pl.program_id(0)
    j = pl.program_id(1)
    do_rope = jnp.logical_and(j >= QK_NA_SLOTS, i < T_LAT // tm)
    for hh in range(QK_PAIR):
        sl = slice(hh * HD, (hh + 1) * HD)
        y = x_ref[:, sl].astype(F32)
        ms = jnp.mean(y * y, axis=-1, keepdims=True)
        yn = y * lax.rsqrt(ms + NORM_EPS) * gain_ref[0]
        lane = lax.broadcasted_iota(jnp.int32, yn.shape, 1)
        nxt = pltpu.roll(yn, LANES - 1, 1)
        prv = pltpu.roll(yn, 1, 1)
        partner = jnp.where(lane % 2 == 0, nxt, prv)
        yr = yn * cos_ref[...] + partner * sin_ref[...]
        o_ref[:, sl] = jnp.where(do_rope, yr, yn).astype(o_ref.dtype)


def _qk_prep(qkv, gain, cos_rep, sin_signed):
    tm = TM
    per_b = S // tm
    wide = QK_PAIR * HD
    return pl.pallas_call(
        functools.partial(_qk_kernel, tm=tm),
        out_shape=jax.ShapeDtypeStruct(qkv.shape, qkv.dtype),
        grid=(T_ALL // tm, N_QK_SLOTS),
        in_specs=[pl.BlockSpec((tm, wide), lambda i, j: (i, _qk_col(j))),
                  pl.BlockSpec((1, 1, HD), lambda i, j: (_qk_gain_row(j), 0, 0)),
                  pl.BlockSpec((tm, HD), lambda i, j: (i % per_b, 0)),
                  pl.BlockSpec((tm, HD), lambda i, j: (i % per_b, 0))],
        out_specs=pl.BlockSpec((tm, wide), lambda i, j: (i, _qk_col(j))),
        input_output_aliases={0: 0},
        compiler_params=pltpu.CompilerParams(dimension_semantics=("parallel", "parallel")),
        name="qk_prep",
    )(qkv, gain, cos_rep, sin_signed)


def _rope_tables():
    t = np.arange(S)
    row = (t // GRID_W).astype(np.float32)
    col = (t % GRID_W).astype(np.float32)
    axis_dim = HD // 2
    inv_freq = jnp.asarray(ROPE_THETA, F32) ** (-jnp.arange(0, axis_dim, 2, dtype=F32) / axis_dim)
    ang = jnp.concatenate([jnp.asarray(row)[:, None] * inv_freq, jnp.asarray(col)[:, None] * inv_freq], axis=-1)
    cos, sin = jnp.cos(ang), jnp.sin(ang)
    cos_rep = jnp.repeat(cos, 2, axis=-1)
    sin_signed = jnp.stack([-sin, sin], axis=-1).reshape(S, HD)
    return cos_rep, sin_signed


COL_NA_Q, COL_NA_K, COL_NA_V = 0, NA_H, 2 * NA_H
COL_G_Q, COL_G_K, COL_G_V = 3 * NA_H, 3 * NA_H + GQA_H, 3 * NA_H + GQA_H + GQA_KV
ATT_SCALE = HD ** -0.5
_NT = (((1,), (1,)), ((), ()))


def _attn_kernel(*refs, two):
    if two:
        q_ref, k1_ref, v1_ref, k2_ref, v2_ref, _, o_ref = refs
    else:
        q_ref, k1_ref, v1_ref, _, o_ref = refs
    q = q_ref[...]
    s1 = lax.dot_general(q, k1_ref[...], _NT, preferred_element_type=F32) * ATT_SCALE
    m = jnp.max(s1, axis=-1, keepdims=True)
    if two:
        s2 = lax.dot_general(q, k2_ref[...], _NT, preferred_element_type=F32) * ATT_SCALE
        m = jnp.maximum(m, jnp.max(s2, axis=-1, keepdims=True))
    p1 = jnp.exp(s1 - m)
    l = jnp.sum(p1, axis=-1, keepdims=True)
    o = jnp.dot(p1.astype(BF16), v1_ref[...], preferred_element_type=F32)
    if two:
        p2 = jnp.exp(s2 - m)
        l = l + jnp.sum(p2, axis=-1, keepdims=True)
        o = o + jnp.dot(p2.astype(BF16), v2_ref[...], preferred_element_type=F32)
    o_ref[...] = (o / l).astype(o_ref.dtype)


def _gqa_lat(qkv, o_buf):
    tq = 512
    nq = S // tq
    ctx_blk = T_LAT // LC
    grp = GQA_H // GQA_KV
    return pl.pallas_call(
        functools.partial(_attn_kernel, two=True),
        out_shape=jax.ShapeDtypeStruct(o_buf.shape, o_buf.dtype),
        grid=(B, GQA_H, nq),
        in_specs=[pl.BlockSpec((tq, HD), lambda b, h, t: (b * nq + t, COL_G_Q + h)),
                  pl.BlockSpec((LC, HD), lambda b, h, t: (ctx_blk + b, COL_G_K + h // grp)),
                  pl.BlockSpec((LC, HD), lambda b, h, t: (ctx_blk + b, COL_G_V + h // grp)),
                  pl.BlockSpec((S, HD), lambda b, h, t: (b, COL_G_K + h // grp)),
                  pl.BlockSpec((S, HD), lambda b, h, t: (b, COL_G_V + h // grp)),
                  pl.BlockSpec(memory_space=pl.ANY)],
        out_specs=pl.BlockSpec((tq, HD), lambda b, h, t: (b * nq + t, NA_H + h)),
        input_output_aliases={5: 0},
        compiler_params=pltpu.CompilerParams(dimension_semantics=("parallel", "parallel", "parallel")),
        name="gqa_lat",
    )(qkv, qkv, qkv, qkv, qkv, o_buf)


def _ctx_attn(qkv, o_buf):
    ctx_blk = T_LAT // LC
    grp = GQA_H // GQA_KV

    def qcol(h):
        return jnp.where(h < NA_H, COL_NA_Q + h, COL_G_Q + h - NA_H)

    def kcol(h):
        return jnp.where(h < NA_H, COL_NA_K + h, COL_G_K + (h - NA_H) // grp)

    def vcol(h):
        return jnp.where(h < NA_H, COL_NA_V + h, COL_G_V + (h - NA_H) // grp)

    return pl.pallas_call(
        functools.partial(_attn_kernel, two=False),
        out_shape=jax.ShapeDtypeStruct(o_buf.shape, o_buf.dtype),
        grid=(B, NA_H + GQA_H),
        in_specs=[pl.BlockSpec((LC, HD), lambda b, h: (ctx_blk + b, qcol(h))),
                  pl.BlockSpec((LC, HD), lambda b, h: (ctx_blk + b, kcol(h))),
                  pl.BlockSpec((LC, HD), lambda b, h: (ctx_blk + b, vcol(h))),
                  pl.BlockSpec(memory_space=pl.ANY)],
        out_specs=pl.BlockSpec((LC, HD), lambda b, h: (ctx_blk + b, h)),
        input_output_aliases={3: 0},
        compiler_params=pltpu.CompilerParams(dimension_semantics=("parallel", "parallel")),
        name="ctx_attn",
    )(qkv, qkv, qkv, o_buf)


def _na_kernel(q_ref, k_ref, v_ref, kc_ref, vc_ref, bias_ref, _, o_ref):
    kc = kc_ref[...]
    vc = vc_ref[...]
    n_win = NA_KH * GRID_W

    def body(r, carry):
        r0 = jnp.clip(r - NA_KH // 2, 0, ROWS - NA_KH)
        rel0 = r0 - r + NA_KH - 1
        q = q_ref[pl.ds(pl.multiple_of(r * GRID_W, GRID_W), GRID_W), :]
        kw = k_ref[pl.ds(pl.multiple_of(r0 * GRID_W, GRID_W), n_win), :]
        vw = v_ref[pl.ds(pl.multiple_of(r0 * GRID_W, GRID_W), n_win), :]
        sw = lax.dot_general(q, kw, _NT, preferred_element_type=F32) * ATT_SCALE + bias_ref[0, rel0]
        sc = lax.dot_general(q, kc, _NT, preferred_element_type=F32) * ATT_SCALE
        m = jnp.maximum(jnp.max(sw, axis=-1, keepdims=True), jnp.max(sc, axis=-1, keepdims=True))
        pw = jnp.exp(sw - m)
        pc = jnp.exp(sc - m)
        l = jnp.sum(pw, axis=-1, keepdims=True) + jnp.sum(pc, axis=-1, keepdims=True)
        o = (jnp.dot(pw.astype(BF16), vw, preferred_element_type=F32)
             + jnp.dot(pc.astype(BF16), vc, preferred_element_type=F32))
        o_ref[pl.ds(pl.multiple_of(r * GRID_W, GRID_W), GRID_W), :] = (o / l).astype(o_ref.dtype)
        return carry

    lax.fori_loop(0, ROWS, body, 0, unroll=4)


def _na_bias_table(rpb):
    n_rel = 2 * NA_KW - 1
    qc = np.arange(GRID_W)[:, None]
    kc = np.arange(GRID_W)[None, :]
    start = np.clip(qc - NA_KW // 2, 0, GRID_W - NA_KW)
    valid = (kc >= start) & (kc < start + NA_KW)
    onehot = ((np.arange(n_rel)[:, None, None] == (kc - qc + NA_KW - 1)[None]) & valid[None]).astype(np.float32)
    exp = jnp.einsum("hrc,cqk->hqrk", rpb, jnp.asarray(onehot), precision=lax.Precision.HIGHEST)
    exp = exp + jnp.asarray(np.where(valid, 0.0, NEG).astype(np.float32))[None, :, None, :]
    tabs = [exp[:, :, rel0:rel0 + NA_KH, :].reshape(NA_H, GRID_W, NA_KH * GRID_W) for rel0 in range(NA_KH)]
    return jnp.stack(tabs, axis=1)


def _na_lat(qkv, bias_tab, o_buf):
    ctx_blk = T_LAT // LC
    return pl.pallas_call(
        _na_kernel,
        out_shape=jax.ShapeDtypeStruct(o_buf.shape, o_buf.dtype),
        grid=(B, NA_H),
        in_specs=[pl.BlockSpec((S, HD), lambda b, h: (b, COL_NA_Q + h)),
                  pl.BlockSpec((S, HD), lambda b, h: (b, COL_NA_K + h)),
                  pl.BlockSpec((S, HD), lambda b, h: (b, COL_NA_V + h)),
                  pl.BlockSpec((LC, HD), lambda b, h: (ctx_blk + b, COL_NA_K + h)),
                  pl.BlockSpec((LC, HD), lambda b, h: (ctx_blk + b, COL_NA_V + h)),
                  pl.BlockSpec((1, NA_KH, GRID_W, NA_KH * GRID_W), lambda b, h: (h, 0, 0, 0)),
                  pl.BlockSpec(memory_space=pl.ANY)],
        out_specs=pl.BlockSpec((S, HD), lambda b, h: (b, h)),
        input_output_aliases={6: 0},
        compiler_params=pltpu.CompilerParams(dimension_semantics=("parallel", "parallel")),
        name="na_lat",
    )(qkv, qkv, qkv, qkv, qkv, bias_tab, o_buf)


def _gate_kernel(a_ref, w_ref, b_ref, o_ref):
    g = jnp.dot(a_ref[...], w_ref[...].astype(BF16), preferred_element_type=F32) + b_ref[...]
    g = GATE_SOFTCAP * jnp.tanh(g / GATE_SOFTCAP)
    lane = lax.broadcasted_iota(jnp.int32, g.shape, 1)
    is_forget = (lane // ML_H) % 2 == 1
    log_sig = jnp.minimum(g, 0.0) - jnp.log(1.0 + jnp.exp(-jnp.abs(g)))
    o_ref[...] = jnp.where(is_forget, log_sig, g)


def _mlstm_gates(h, wg, bg):
    tm = TM
    return pl.pallas_call(
        _gate_kernel,
        out_shape=jax.ShapeDtypeStruct((T_ALL, LANES), F32),
        grid=(T_ALL // tm,),
        in_specs=[pl.BlockSpec((tm, D), lambda i: (i, 0)),
                  pl.BlockSpec((D, LANES), lambda i: (0, 0)),
                  pl.BlockSpec((1, LANES), lambda i: (0, 0))],
        out_specs=pl.BlockSpec((tm, LANES), lambda i: (i, 0)),
        compiler_params=pltpu.CompilerParams(dimension_semantics=("parallel",)),
        name="mlstm_gates",
    )(h, wg, bg)


ML_SCALE = ML_DK ** -0.5
_TN = (((0,), (0,)), ((), ()))


def _mlstm_chunk(q, k, v, ig, fg, c_st, n_st, m_st, rev):
    L = q.shape[0]
    ti = lax.broadcasted_iota(jnp.int32, (L, L), 0)
    si = lax.broadcasted_iota(jnp.int32, (L, L), 1)
    eye = ti == si
    before_col = (si >= ti) if rev else (si <= ti)
    before_row = (ti >= si) if rev else (ti <= si)
    f_col = jnp.sum(jnp.where(eye, fg, 0.0), axis=1, keepdims=True)
    i_col = jnp.sum(jnp.where(eye, ig, 0.0), axis=1, keepdims=True)
    b_col = jnp.sum(jnp.where(before_col, fg, 0.0), axis=1, keepdims=True)
    b_row = jnp.sum(jnp.where(before_row, f_col, 0.0), axis=0, keepdims=True)
    total = jnp.sum(fg, axis=1, keepdims=True)
    g_col = total - b_col + i_col
    m_new = jnp.maximum(total + m_st, jnp.max(g_col, axis=0, keepdims=True))
    decay = jnp.exp(total + m_st - m_new)
    wk = jnp.exp(g_col - m_new)
    kw = k.astype(F32) * wk
    c_new = decay * c_st + lax.dot_general(kw.astype(BF16), v, _TN, preferred_element_type=F32)
    n_new = decay * n_st + jnp.sum(kw, axis=0, keepdims=True)

    dmat = jnp.where(before_col, b_col - b_row + ig, NEG)
    inter = b_col + m_st
    m_t = jnp.maximum(inter, jnp.max(dmat, axis=1, keepdims=True))
    a = jnp.exp(inter - m_t)
    qk = lax.dot_general(q, k, _NT, preferred_element_type=F32) * ML_SCALE
    smat = qk * jnp.exp(dmat - m_t)
    num = (a * (jnp.dot(q, c_st.astype(BF16), preferred_element_type=F32) * ML_SCALE)
           + jnp.dot(smat.astype(BF16), v, preferred_element_type=F32))
    den = (a * (jnp.sum(q.astype(F32) * n_st, axis=1, keepdims=True) * ML_SCALE)
           + jnp.sum(smat, axis=1, keepdims=True))
    h = num / jnp.maximum(jnp.abs(den), jnp.exp(-m_t))
    return h, c_new, n_new, m_new


def _mlstm_kernel(qf, kf, vf, gf, qb, kb, vb, gb, hf_ref, hb_ref, cf, nf, mf, cb, nb, mb):
    @pl.when(pl.program_id(1) == 0)
    def _():
        for r in (cf, nf, mf, cb, nb, mb):
            r[...] = jnp.zeros(r.shape, r.dtype)

    gfv = gf[0]
    h, c_new, n_new, m_new = _mlstm_chunk(qf[...], kf[...], vf[...], gfv[0:1], gfv[1:2],
                                          cf[...], nf[...], mf[...], False)
    hf_ref[...] = h
    cf[...] = c_new
    nf[...] = n_new
    mf[...] = m_new
    gbv = gb[0]
    h, c_new, n_new, m_new = _mlstm_chunk(qb[...], kb[...], vb[...], gbv[2:3], gbv[3:4],
                                          cb[...], nb[...], mb[...], True)
    hb_ref[...] = h
    cb[...] = c_new
    nb[...] = n_new
    mb[...] = m_new


def _mlstm_scan(proj, gates):
    L = ML_CHUNK
    n_lat = S // L
    assert LC == L
    ctx_blk = T_LAT // L
    qcol, kcol, vcol = 0, ML_H, (2 * ML_H * ML_DK) // ML_DV

    def fwd(bh, c):
        b = bh // ML_H
        return jnp.where(c == 0, ctx_blk + b, b * n_lat + c - 1)

    def bwd(bh, c):
        b = bh // ML_H
        return jnp.where(c == 0, ctx_blk + b, b * n_lat + n_lat - c)

    def specs(blk):
        return [pl.BlockSpec((L, ML_DK), lambda bh, c: (blk(bh, c), qcol + bh % ML_H)),
                pl.BlockSpec((L, ML_DK), lambda bh, c: (blk(bh, c), kcol + bh % ML_H)),
                pl.BlockSpec((L, ML_DV), lambda bh, c: (blk(bh, c), vcol + bh % ML_H)),
                pl.BlockSpec((1, 4, L), lambda bh, c: (bh % ML_H, 0, blk(bh, c)))]

    out_sds = jax.ShapeDtypeStruct((T_ALL, ML_H * ML_DV), F32)
    return pl.pallas_call(
        _mlstm_kernel,
        out_shape=[out_sds, out_sds],
        grid=(B * ML_H, 1 + n_lat),
        in_specs=specs(fwd) + specs(bwd),
        out_specs=[pl.BlockSpec((L, ML_DV), lambda bh, c: (fwd(bh, c), bh % ML_H)),
                   pl.BlockSpec((L, ML_DV), lambda bh, c: (bwd(bh, c), bh % ML_H))],
        scratch_shapes=[pltpu.VMEM((ML_DK, ML_DV), F32), pltpu.VMEM((1, ML_DK), F32), pltpu.VMEM((1, 1), F32),
                        pltpu.VMEM((ML_DK, ML_DV), F32), pltpu.VMEM((1, ML_DK), F32), pltpu.VMEM((1, 1), F32)],
        compiler_params=pltpu.CompilerParams(dimension_semantics=("parallel", "arbitrary")),
        name="mlstm_scan",
    )(proj, proj, proj, gates, proj, proj, proj, gates)


def _readout_kernel(hf_ref, hb_ref, o_ref, gain_ref, out_ref):
    hs = hf_ref[...] + hb_ref[...]
    o = o_ref[...].astype(F32)
    for hh in range(ML_H):
        sl = slice(hh * ML_DV, (hh + 1) * ML_DV)
        x = hs[:, sl]
        ms = jnp.mean(x * x, axis=-1, keepdims=True)
        y = x * lax.rsqrt(ms + NORM_EPS) * gain_ref[:, sl]
        og = o[:, sl]
        out_ref[:, sl] = (y * (1.0 / (1.0 + jnp.exp(-og)))).astype(out_ref.dtype)


def _mlstm_readout(hf, hb, proj, gain, n_rows):
    tm = TNORM
    wide = ML_H * ML_DV
    ocol = (2 * ML_H * ML_DK + ML_H * ML_DV) // wide
    return pl.pallas_call(
        _readout_kernel,
        out_shape=jax.ShapeDtypeStruct((n_rows, wide), BF16),
        grid=(n_rows // tm,),
        in_specs=[pl.BlockSpec((tm, wide), lambda i: (i, 0)),
                  pl.BlockSpec((tm, wide), lambda i: (i, 0)),
                  pl.BlockSpec((tm, wide), lambda i: (i, ocol)),
                  pl.BlockSpec((1, wide), lambda i: (0, 0))],
        out_specs=pl.BlockSpec((tm, wide), lambda i: (i, 0)),
        compiler_params=pltpu.CompilerParams(dimension_semantics=("parallel",)),
        name="mlstm_readout",
    )(hf, hb, proj, gain)


def _moe_kernel(be_ref, nu_ref, order_ref, nv_ref, shift_ref, h_hbm, w1_ref, b1_ref, w2_ref, b2_ref, sel_ref,
                y_ref, xf_ref, xb_ref, sem):
    i = pl.program_id(0)
    f = pl.program_id(1)
    n_blocks = pl.num_programs(0)
    nf = pl.num_programs(1)
    bm = xb_ref.shape[0]
    per_tile = bm // MOE_NF
    n_used = nu_ref[0]

    def row_copy(blk, r):
        src = jnp.minimum(r + shift_ref[blk], order_ref.shape[0] - 1)
        tok = jnp.where(r < nv_ref[blk], order_ref[src] // TOP_K, 0)
        return pltpu.make_async_copy(h_hbm.at[pl.ds(tok, 1)], xf_ref.at[pl.ds(r, 1)], sem)

    def wait_block(blk):
        def body(r, carry):
            row_copy(blk, r).wait()
            return carry
        lax.fori_loop(0, bm, body, 0, unroll=8)

    @pl.when(jnp.logical_and(i == 0, f == 0))
    def _():
        def body(r, carry):
            row_copy(0, r).start()
            return carry
        lax.fori_loop(0, bm, body, 0, unroll=8)

    @pl.when(jnp.logical_and(f == 0, i <= n_used))
    def _():
        wait_block(i)
        xb_ref[...] = xf_ref[...].astype(BF16)
        y_ref[...] = jnp.broadcast_to(b2_ref[0], y_ref.shape)

    def compute(m):
        nxt = jnp.minimum(i + 1, n_blocks - 1)
        for r in range(per_tile):
            row_copy(nxt, f * per_tile + r).start()
        h = jnp.dot(xb_ref[0:m, :], w1_ref[0].astype(BF16), preferred_element_type=F32) + b1_ref[0]
        glu = jnp.minimum(h, SWIGLU_LIMIT)
        glu = glu * (1.0 / (1.0 + jnp.exp(-SWIGLU_ALPHA * glu)))
        lin = jnp.clip(h, -SWIGLU_LIMIT, SWIGLU_LIMIT) + 1.0
        prod = glu * pltpu.roll(lin, h.shape[1] - 1, 1)
        act = jnp.dot(prod.astype(BF16), sel_ref[...], preferred_element_type=F32)
        y_ref[0:m, :] += jnp.dot(act.astype(BF16), w2_ref[0].astype(BF16), preferred_element_type=F32)

    n_valid = jnp.where(i < n_used, nv_ref[jnp.minimum(i, n_blocks - 1)], 0)
    for k in range(1, bm // MOE_SUB + 1):
        @pl.when(jnp.logical_and(n_valid > (k - 1) * MOE_SUB, n_valid <= k * MOE_SUB))
        def _(k=k):
            compute(k * MOE_SUB)

    @pl.when(jnp.logical_and(jnp.logical_and(i == n_blocks - 1, f == nf - 1), i < n_used))
    def _():
        wait_block(i)

    @pl.when(jnp.logical_and(i >= n_used, f == 0))
    def _():
        y_ref[...] = jnp.zeros(y_ref.shape, y_ref.dtype)


def _moe_blocks_max(n_tok):
    m = n_tok * TOP_K
    return -(-(m + N_EXP * (MOE_BM - 1)) // MOE_BM)


def _moe_experts(h, order, block_expert, block_valid, block_shift, n_used, layer, w1, b1, w2, b2, sel):
    bm, fh = MOE_BM, MOE_FH
    n_blocks = block_expert.shape[0]
    nf = D_EXP // fh
    e0 = layer * N_EXP

    def wmap(i, nu, be):
        return be[jnp.minimum(i, nu[0] - 1)]

    grid_spec = pltpu.PrefetchScalarGridSpec(
        num_scalar_prefetch=5,
        grid=(n_blocks, nf),
        in_specs=[pl.BlockSpec(memory_space=pl.ANY),
                  pl.BlockSpec((1, D, 2 * fh), lambda i, f, be, nu, od, nv, sh: (e0 + wmap(i, nu, be), 0, f)),
                  pl.BlockSpec((1, 1, 2 * fh), lambda i, f, be, nu, od, nv, sh: (wmap(i, nu, be), 0, f)),
                  pl.BlockSpec((1, fh, D), lambda i, f, be, nu, od, nv, sh: (e0 + wmap(i, nu, be), f, 0)),
                  pl.BlockSpec((1, 1, D), lambda i, f, be, nu, od, nv, sh: (wmap(i, nu, be), 0, 0)),
                  pl.BlockSpec((2 * fh, fh), lambda i, f, be, nu, od, nv, sh: (0, 0))],
        out_specs=pl.BlockSpec((bm, D), lambda i, f, be, nu, od, nv, sh: (i, 0)),
        scratch_shapes=[pltpu.VMEM((bm, D), F32), pltpu.VMEM((bm, D), BF16), pltpu.SemaphoreType.DMA],
    )
    return pl.pallas_call(
        _moe_kernel,
        out_shape=jax.ShapeDtypeStruct((n_blocks * bm, D), F32),
        grid_spec=grid_spec,
        compiler_params=pltpu.CompilerParams(dimension_semantics=("arbitrary", "arbitrary")),
        name="moe_experts",
    )(block_expert, n_used, order, block_valid, block_shift, h, w1, b1, w2, b2, sel)


def _combine_kernel(dest_ref, x_ref, y_hbm, p_ref, gate_ref, o_ref, buf, sem):
    i = pl.program_id(0)
    tm = x_ref.shape[0]

    def row_copy(kk, r):
        d = dest_ref[(i * tm + r) * TOP_K + kk]
        return pltpu.make_async_copy(y_hbm.at[pl.ds(d, 1)], buf.at[kk, pl.ds(r, 1)], sem.at[kk])

    for kk in range(TOP_K):
        def start(r, carry, kk=kk):
            row_copy(kk, r).start()
            return carry
        lax.fori_loop(0, tm, start, 0, unroll=8)

    p = p_ref[...]
    acc = None
    for kk in range(TOP_K):
        def wait(r, carry, kk=kk):
            row_copy(kk, r).wait()
            return carry
        lax.fori_loop(0, tm, wait, 0, unroll=8)
        term = p[:, kk:kk + 1] * buf[kk]
        acc = term if acc is None else acc + term
    o_ref[...] = x_ref[...] + gate_ref[0] * acc


def _moe_combine(xs, y_rows, dest, probs, mod, gate_chunk, n_rows):
    tm = 256
    grid_spec = pltpu.PrefetchScalarGridSpec(
        num_scalar_prefetch=1,
        grid=(n_rows // tm,),
        in_specs=[pl.BlockSpec((tm, D), lambda i, dst: (i, 0)),
                  pl.BlockSpec(memory_space=pl.ANY),
                  pl.BlockSpec((tm, LANES), lambda i, dst: (i, 0)),
                  pl.BlockSpec((1, 1, D), lambda i, dst: (_mod_row(i, tm), 0, gate_chunk))],
        out_specs=pl.BlockSpec((tm, D), lambda i, dst: (i, 0)),
        scratch_shapes=[pltpu.VMEM((TOP_K, tm, D), F32), pltpu.SemaphoreType.DMA((TOP_K,))],
    )
    return pl.pallas_call(
        _combine_kernel,
        out_shape=jax.ShapeDtypeStruct((n_rows, D), F32),
        grid_spec=grid_spec,
        compiler_params=pltpu.CompilerParams(dimension_semantics=("arbitrary",)),
        name="moe_combine",
    )(dest, xs, y_rows, probs, mod)


def _moe_route(top_e, n_tok):
    m = n_tok * TOP_K
    bm = MOE_BM
    n_blocks = _moe_blocks_max(n_tok)
    e_flat = top_e.reshape(m)
    onehot = (e_flat[:, None] == jnp.arange(N_EXP, dtype=jnp.int32)[None, :]).astype(jnp.int32)
    csum = jnp.cumsum(onehot, axis=0)
    counts = csum[-1]
    rank = jnp.take_along_axis(csum, e_flat[:, None], axis=1)[:, 0] - 1
    padded = (counts + bm - 1) // bm * bm
    pad_end = jnp.cumsum(padded)
    pad_start = pad_end - padded
    dest = pad_start[e_flat] + rank
    n_used = (pad_end[-1] // bm).astype(jnp.int32).reshape(1)
    starts = jnp.arange(n_blocks, dtype=jnp.int32) * bm
    block_expert = jnp.minimum(jnp.sum((pad_end[None, :] <= starts[:, None]).astype(jnp.int32), axis=1),
                               N_EXP - 1).astype(jnp.int32)
    order = jnp.argsort(e_flat, stable=True).astype(jnp.int32)
    sort_start = jnp.cumsum(counts) - counts
    block_valid = jnp.clip((pad_start + counts)[block_expert] - starts, 0, bm).astype(jnp.int32)
    block_shift = jnp.maximum(sort_start[block_expert] + starts - pad_start[block_expert], 0).astype(jnp.int32)
    return dest, order, block_expert, block_valid, block_shift, n_used


def _moe_layer(xs, n_rows, g, mod, router_w, router_b, layer, w1_all, b1, w2_all, b2, sel, name):
    rw = jnp.pad(router_w, ((0, 0), (0, LANES - N_EXP)))
    rb = jnp.pad(router_b, (0, LANES - N_EXP), constant_values=NEG).reshape(1, LANES)
    h, top_e, probs = _norm_mod(xs, g, mod, 3, 4, n_rows=n_rows, router_w=rw, router_b=rb, name=name + "_norm")
    dest, order, block_expert, block_valid, block_shift, n_used = _moe_route(top_e[:, :TOP_K], n_rows)
    y_rows = _moe_experts(h, order, block_expert, block_valid, block_shift, n_used, layer, w1_all,
                          b1.reshape(N_EXP, 1, 2 * D_EXP), w2_all, b2.reshape(N_EXP, 1, D), sel)
    return _moe_combine(xs, y_rows, dest, probs, mod, 5, n_rows)


def _select_matrix():
    r = np.arange(2 * MOE_FH)[:, None]
    c = np.arange(MOE_FH)[None, :]
    return jnp.asarray(r == 2 * c, BF16)


def kernel(x, c, ctx, c_ctx, ada_w, ada_b, norm_g, attn_w_in, attn_w_out, attn_qk_gain, na_rpb, mlstm_w_in,
           mlstm_gate_bias, mlstm_head_gain, mlstm_w_out, router_w, router_b, expert_w1, expert_b1,
           expert_w2, expert_b2):
    xs = jnp.concatenate([x.reshape(T_LAT, D), ctx.reshape(T_CTX, D)], axis=0)
    cc = jnp.zeros((8, D), F32).at[:B].set(c).at[MOD_CTX_ROW].set(c_ctx)
    sel = _select_matrix()

    n_layers = ada_w.shape[0]
    ada_w2 = ada_w.reshape(n_layers * D, 6 * D)
    w1_all = expert_w1.reshape(n_layers * N_EXP, D, 2 * D_EXP)
    w2_all = expert_w2.reshape(n_layers * N_EXP, D_EXP, D)

    def ada(layer):
        m = _matmul(cc, ada_w2, 6 * D, out_dtype=F32, tm=8, w_row_block=layer, silu=True,
                    bias=ada_b[layer].reshape(1, 6 * D), name="adaln")
        return m.reshape(8, 1, 6 * D)

    mod = ada(0)
    h = _norm_mod(xs, norm_g[0, 0].reshape(1, D), mod, 0, 1, n_rows=T_ALL, name="l0_norm1")
    qkv = _matmul(h, attn_w_in[0], ATTN_IN, out_dtype=BF16, name="attn_in")
    cos_rep, sin_signed = _rope_tables()
    qkv = _qk_prep(qkv, attn_qk_gain[0].reshape(4, 1, HD), cos_rep, sin_signed)
    o_buf = _na_lat(qkv, _na_bias_table(na_rpb[0]), jnp.zeros((T_ALL, D), BF16))
    o_buf = _gqa_lat(qkv, o_buf)
    o_buf = _ctx_attn(qkv, o_buf)
    xs = _matmul(o_buf, attn_w_out[0], D, out_dtype=F32, res=xs, mod=mod, gate_chunk=2, name="attn_out")
    xs = _moe_layer(xs, T_ALL, norm_g[0, 1].reshape(1, D), mod, router_w[0], router_b[0],
                    0, w1_all, expert_b1[0], w2_all, expert_b2[0], sel, "l0_moe")

    mod = ada(1)
    h = _norm_mod(xs, norm_g[1, 0].reshape(1, D), mod, 0, 1, n_rows=T_ALL, name="l1_norm1")
    proj = _matmul(h, mlstm_w_in[0], ML_MAIN, out_dtype=BF16, name="mlstm_in")
    wg = jnp.pad(mlstm_w_in[0][:, ML_MAIN:], ((0, 0), (0, LANES - 4 * ML_H)))
    bg = jnp.pad(mlstm_gate_bias[0].reshape(-1), (0, LANES - 4 * ML_H)).reshape(1, LANES)
    g = _mlstm_gates(h, wg, bg)
    gates = g[:, :4 * ML_H].T.reshape(4, ML_H, T_ALL).transpose(1, 0, 2)
    hf, hb = _mlstm_scan(proj, gates)
    hn = _mlstm_readout(hf, hb, proj, mlstm_head_gain[0].reshape(1, ML_H * ML_DV), T_LAT)
    xs = _matmul(hn, mlstm_w_out[0], D, out_dtype=F32, m_rows=T_LAT, res=xs, mod=mod, gate_chunk=2,
                 name="mlstm_out")
    out = _moe_layer(xs, T_LAT, norm_g[1, 1].reshape(1, D), mod, router_w[1], router_b[1],
                     1, w1_all, expert_b1[1], w2_all, expert_b2[1], sel, "l1_moe")
    return out.reshape(B, S, D)
```

```python
import functools

import jax
import jax.numpy as jnp
import numpy as np
from jax import lax
from jax.experimental import pallas as pl
from jax.experimental.pallas import tpu as pltpu

F32 = jnp.float32
BF16 = jnp.bfloat16

D = 2048
B = 4
S = 2048
LC = 256
GRID_W = 64
ROWS = S // GRID_W
HD = 128
NA_H = 8
GQA_H = 8
GQA_KV = 2
NA_KH = 8
NA_KW = 16
ROPE_THETA = 10000.0
ML_H = 8
ML_DK = 128
ML_DV = 256
GATE_SOFTCAP = 15.0
N_EXP = 32
TOP_K = 4
D_EXP = D
SWIGLU_ALPHA = 1.702
SWIGLU_LIMIT = 7.0
NORM_EPS = 1e-6
ATTN_IN = 3 * NA_H * HD + GQA_H * HD + 2 * GQA_KV * HD
ML_MAIN = 2 * ML_H * ML_DK + 2 * ML_H * ML_DV

T_LAT = B * S
T_CTX = B * LC
T_ALL = T_LAT + T_CTX
MOD_CTX_ROW = B

LANES = 128

TM = 1024
TN = 512
TNORM = 512
ML_CHUNK = 256
MOE_BM = 512
MOE_FH = 512
MOE_SUB = 128
NEG = -1e30


def _mod_row(i, tm):
    n_lat = T_LAT // tm
    per_b = S // tm
    return jnp.where(i < n_lat, i // per_b, MOD_CTX_ROW)


def _mm_kernel(*refs, silu, has_bias, has_res):
    a_ref, w_ref = refs[0], refs[1]
    pos = 2
    bias_ref = res_ref = gate_ref = None
    if has_bias:
        bias_ref = refs[pos]
        pos += 1
    if has_res:
        res_ref, gate_ref = refs[pos], refs[pos + 1]
        pos += 2
    o_ref = refs[pos]
    a = a_ref[...]
    if silu:
        a = a.astype(F32)
        a = a * (1.0 / (1.0 + jnp.exp(-a)))
    acc = jnp.dot(a.astype(BF16), w_ref[...].astype(BF16), preferred_element_type=F32)
    if has_bias:
        acc = acc + bias_ref[...]
    if has_res:
        acc = res_ref[...] + gate_ref[0] * acc
    o_ref[...] = acc.astype(o_ref.dtype)


def _matmul(a, w, n_out, *, out_dtype, tm=TM, tn=TN, m_rows=None, w_row_block=0, silu=False, bias=None,
            res=None, mod=None, gate_chunk=None, name="mm"):
    m_rows = a.shape[0] if m_rows is None else m_rows
    k = a.shape[1]
    grid = (m_rows // tm, n_out // tn)
    in_specs = [pl.BlockSpec((tm, k), lambda i, j: (i, 0)),
                pl.BlockSpec((k, tn), lambda i, j: (w_row_block, j))]
    args = [a, w]
    if bias is not None:
        in_specs.append(pl.BlockSpec((1, tn), lambda i, j: (0, j)))
        args.append(bias)
    if res is not None:
        per = D // tn
        in_specs.append(pl.BlockSpec((tm, tn), lambda i, j: (i, j)))
        in_specs.append(pl.BlockSpec((1, 1, tn), lambda i, j: (_mod_row(i, tm), 0, gate_chunk * per + j)))
        args += [res, mod]
    return pl.pallas_call(
        functools.partial(_mm_kernel, silu=silu, has_bias=bias is not None, has_res=res is not None),
        out_shape=jax.ShapeDtypeStruct((m_rows, n_out), out_dtype),
        grid=grid, in_specs=in_specs,
        out_specs=pl.BlockSpec((tm, tn), lambda i, j: (i, j)),
        compiler_params=pltpu.CompilerParams(dimension_semantics=("parallel", "parallel")),
        name=name,
    )(*args)


def _norm_kernel(*refs, router):
    x_ref, g_ref, shift_ref, scale_ref = refs[:4]
    x = x_ref[...]
    ms = jnp.mean(x * x, axis=-1, keepdims=True)
    y = x * lax.rsqrt(ms + NORM_EPS) * g_ref[...]
    h = y * (1.0 + scale_ref[0]) + shift_ref[0]
    hb = h.astype(BF16)
    if not router:
        refs[4][...] = hb
        return
    rw_ref, rb_ref, h_ref, e_ref, p_ref = refs[4:]
    h_ref[...] = h
    logits = jnp.dot(hb, rw_ref[...].astype(BF16), preferred_element_type=F32) + rb_ref[...]
    lane = lax.broadcasted_iota(jnp.int32, logits.shape, 1)
    e_out = jnp.zeros(logits.shape, jnp.int32)
    v_out = jnp.full(logits.shape, NEG, F32)
    work = logits
    for kk in range(TOP_K):
        mx = jnp.max(work, axis=-1, keepdims=True)
        idx = jnp.min(jnp.where(work == mx, lane, LANES), axis=-1, keepdims=True)
        e_out = jnp.where(lane == kk, idx, e_out)
        v_out = jnp.where(lane == kk, mx, v_out)
        work = jnp.where(lane == idx, -jnp.inf, work)
    top0 = jnp.max(v_out, axis=-1, keepdims=True)
    pe = jnp.exp(v_out - top0)
    e_ref[...] = e_out
    p_ref[...] = pe / jnp.sum(pe, axis=-1, keepdims=True)


def _norm_mod(xs, g, mod, shift_chunk, scale_chunk, *, n_rows, router_w=None, router_b=None, name="norm"):
    tm = TNORM
    in_specs = [pl.BlockSpec((tm, D), lambda i: (i, 0)),
                pl.BlockSpec((1, D), lambda i: (0, 0)),
                pl.BlockSpec((1, 1, D), lambda i: (_mod_row(i, tm), 0, shift_chunk)),
                pl.BlockSpec((1, 1, D), lambda i: (_mod_row(i, tm), 0, scale_chunk))]
    args = [xs, g, mod, mod]
    router = router_w is not None
    out_shape = [jax.ShapeDtypeStruct((n_rows, D), F32 if router else BF16)]
    out_specs = [pl.BlockSpec((tm, D), lambda i: (i, 0))]
    if router:
        in_specs += [pl.BlockSpec((D, LANES), lambda i: (0, 0)),
                     pl.BlockSpec((1, LANES), lambda i: (0, 0))]
        args += [router_w, router_b]
        out_shape += [jax.ShapeDtypeStruct((n_rows, LANES), jnp.int32),
                      jax.ShapeDtypeStruct((n_rows, LANES), F32)]
        out_specs += [pl.BlockSpec((tm, LANES), lambda i: (i, 0))] * 2
    out = pl.pallas_call(
        functools.partial(_norm_kernel, router=router),
        out_shape=out_shape, grid=(n_rows // tm,), in_specs=in_specs, out_specs=out_specs,
        compiler_params=pltpu.CompilerParams(dimension_semantics=("parallel",)),
        name=name,
    )(*args)
    return out if router else out[0]


QK_PAIR = 2
N_QK_SLOTS = (2 * NA_H + GQA_H + GQA_KV) // QK_PAIR
QK_NA_SLOTS = 2 * NA_H // QK_PAIR


def _qk_col(j):
    return jnp.where(j < QK_NA_SLOTS, j, j + NA_H // QK_PAIR)


def _qk_gain_row(j):
    half = NA_H // QK_PAIR
    return jnp.where(j < half, 0, jnp.where(j < 2 * half, 1, jnp.where(j < 3 * half, 2, 3)))


def _qk_kernel(x_ref, gain_ref, cos_ref, sin_ref, o_ref, *, tm):
    i = pl.program_id(0)
    j = pl.program_id(1)
    do_rope = jnp.logical_and(j >= QK_NA_SLOTS, i < T_LAT // tm)
    for hh in range(QK_PAIR):
        sl = slice(hh * HD, (hh + 1) * HD)
        y = x_ref[:, sl].astype(F32)
        ms = jnp.mean(y * y, axis=-1, keepdims=True)
        yn = y * lax.rsqrt(ms + NORM_EPS) * gain_ref[0]
        lane = lax.broadcasted_iota(jnp.int32, yn.shape, 1)
        nxt = pltpu.roll(yn, LANES - 1, 1)
        prv = pltpu.roll(yn, 1, 1)
        partner = jnp.where(lane % 2 == 0, nxt, prv)
        yr = yn * cos_ref[...] + partner * sin_ref[...]
        o_ref[:, sl] = jnp.where(do_rope, yr, yn).astype(o_ref.dtype)


def _qk_prep(qkv, gain, cos_rep, sin_signed):
    tm = TM
    per_b = S // tm
    wide = QK_PAIR * HD
    return pl.pallas_call(
        functools.partial(_qk_kernel, tm=tm),
        out_shape=jax.ShapeDtypeStruct(qkv.shape, qkv.dtype),
        grid=(T_ALL // tm, N_QK_SLOTS),
        in_specs=[pl.BlockSpec((tm, wide), lambda i, j: (i, _qk_col(j))),
                  pl.BlockSpec((1, 1, HD), lambda i, j: (_qk_gain_row(j), 0, 0)),
                  pl.BlockSpec((tm, HD), lambda i, j: (i % per_b, 0)),
                  pl.BlockSpec((tm, HD), lambda i, j: (i % per_b, 0))],
        out_specs=pl.BlockSpec((tm, wide), lambda i, j: (i, _qk_col(j))),
        input_output_aliases={0: 0},
        compiler_params=pltpu.CompilerParams(dimension_semantics=("parallel", "parallel")),
        name="qk_prep",
    )(qkv, gain, cos_rep, sin_signed)


def _rope_tables():
    t = np.arange(S)
    row = (t // GRID_W).astype(np.float32)
    col = (t % GRID_W).astype(np.float32)
    axis_dim = HD // 2
    inv_freq = jnp.asarray(ROPE_THETA, F32) ** (-jnp.arange(0, axis_dim, 2, dtype=F32) / axis_dim)
    ang = jnp.concatenate([jnp.asarray(row)[:, None] * inv_freq, jnp.asarray(col)[:, None] * inv_freq], axis=-1)
    cos, sin = jnp.cos(ang), jnp.sin(ang)
    cos_rep = jnp.repeat(cos, 2, axis=-1)
    sin_signed = jnp.stack([-sin, sin], axis=-1).reshape(S, HD)
    return cos_rep, sin_signed


COL_NA_Q, COL_NA_K, COL_NA_V = 0, NA_H, 2 * NA_H
COL_G_Q, COL_G_K, COL_G_V = 3 * NA_H, 3 * NA_H + GQA_H, 3 * NA_H + GQA_H + GQA_KV
ATT_SCALE = HD ** -0.5
_NT = (((1,), (1,)), ((), ()))


def _attn_kernel(*refs, two):
    if two:
        q_ref, k1_ref, v1_ref, k2_ref, v2_ref, _, o_ref = refs
    else:
        q_ref, k1_ref, v1_ref, _, o_ref = refs
    q = q_ref[...]
    s1 = lax.dot_general(q, k1_ref[...], _NT, preferred_element_type=F32) * ATT_SCALE
    m = jnp.max(s1, axis=-1, keepdims=True)
    if two:
        s2 = lax.dot_general(q, k2_ref[...], _NT, preferred_element_type=F32) * ATT_SCALE
        m = jnp.maximum(m, jnp.max(s2, axis=-1, keepdims=True))
    p1 = jnp.exp(s1 - m)
    l = jnp.sum(p1, axis=-1, keepdims=True)
    o = jnp.dot(p1.astype(BF16), v1_ref[...], preferred_element_type=F32)
    if two:
        p2 = jnp.exp(s2 - m)
        l = l + jnp.sum(p2, axis=-1, keepdims=True)
        o = o + jnp.dot(p2.astype(BF16), v2_ref[...], preferred_element_type=F32)
    o_ref[...] = (o / l).astype(o_ref.dtype)


def _gqa_lat(qkv, o_buf):
    tq = 512
    nq = S // tq
    ctx_blk = T_LAT // LC
    grp = GQA_H // GQA_KV
    return pl.pallas_call(
        functools.partial(_attn_kernel, two=True),
        out_shape=jax.ShapeDtypeStruct(o_buf.shape, o_buf.dtype),
        grid=(B, GQA_H, nq),
        in_specs=[pl.BlockSpec((tq, HD), lambda b, h, t: (b * nq + t, COL_G_Q + h)),
                  pl.BlockSpec((LC, HD), lambda b, h, t: (ctx_blk + b, COL_G_K + h // grp)),
                  pl.BlockSpec((LC, HD), lambda b, h, t: (ctx_blk + b, COL_G_V + h // grp)),
                  pl.BlockSpec((S, HD), lambda b, h, t: (b, COL_G_K + h // grp)),
                  pl.BlockSpec((S, HD), lambda b, h, t: (b, COL_G_V + h // grp)),
                  pl.BlockSpec(memory_space=pl.ANY)],
        out_specs=pl.BlockSpec((tq, HD), lambda b, h, t: (b * nq + t, NA_H + h)),
        input_output_aliases={5: 0},
        compiler_params=pltpu.CompilerParams(dimension_semantics=("parallel", "parallel", "parallel")),
        name="gqa_lat",
    )(qkv, qkv, qkv, qkv, qkv, o_buf)


def _ctx_attn(qkv, o_buf):
    ctx_blk = T_LAT // LC
    grp = GQA_H // GQA_KV

    def qcol(h):
        return jnp.where(h < NA_H, COL_NA_Q + h, COL_G_Q + h - NA_H)

    def kcol(h):
        return jnp.where(h < NA_H, COL_NA_K + h, COL_G_K + (h - NA_H) // grp)

    def vcol(h):
        return jnp.where(h < NA_H, COL_NA_V + h, COL_G_V + (h - NA_H) // grp)

    return pl.pallas_call(
        functools.partial(_attn_kernel, two=False),
        out_shape=jax.ShapeDtypeStruct(o_buf.shape, o_buf.dtype),
        grid=(B, NA_H + GQA_H),
        in_specs=[pl.BlockSpec((LC, HD), lambda b, h: (ctx_blk + b, qcol(h))),
                  pl.BlockSpec((LC, HD), lambda b, h: (ctx_blk + b, kcol(h))),
                  pl.BlockSpec((LC, HD), lambda b, h: (ctx_blk + b, vcol(h))),
                  pl.BlockSpec(memory_space=pl.ANY)],
        out_specs=pl.BlockSpec((LC, HD), lambda b, h: (ctx_blk + b, h)),
        input_output_aliases={3: 0},
        compiler_params=pltpu.CompilerParams(dimension_semantics=("parallel", "parallel")),
        name="ctx_attn",
    )(qkv, qkv, qkv, o_buf)


def _na_kernel(q_ref, k_ref, v_ref, kc_ref, vc_ref, bias_ref, _, o_ref):
    kc = kc_ref[...]
    vc = vc_ref[...]
    n_win = NA_KH * GRID_W

    def body(r, carry):
        r0 = jnp.clip(r - NA_KH // 2, 0, ROWS - NA_KH)
        rel0 = r0 - r + NA_KH - 1
        q = q_ref[pl.ds(pl.multiple_of(r * GRID_W, GRID_W), GRID_W), :]
        kw = k_ref[pl.ds(pl.multiple_of(r0 * GRID_W, GRID_W), n_win), :]
        vw = v_ref[pl.ds(pl.multiple_of(r0 * GRID_W, GRID_W), n_win), :]
        sw = lax.dot_general(q, kw, _NT, preferred_element_type=F32) * ATT_SCALE + bias_ref[0, rel0]
        sc = lax.dot_general(q, kc, _NT, preferred_element_type=F32) * ATT_SCALE
        m = jnp.maximum(jnp.max(sw, axis=-1, keepdims=True), jnp.max(sc, axis=-1, keepdims=True))
        pw = jnp.exp(sw - m)
        pc = jnp.exp(sc - m)
        l = jnp.sum(pw, axis=-1, keepdims=True) + jnp.sum(pc, axis=-1, keepdims=True)
        o = (jnp.dot(pw.astype(BF16), vw, preferred_element_type=F32)
             + jnp.dot(pc.astype(BF16), vc, preferred_element_type=F32))
        o_ref[pl.ds(pl.multiple_of(r * GRID_W, GRID_W), GRID_W), :] = (o / l).astype(o_ref.dtype)
        return carry

    lax.fori_loop(0, ROWS, body, 0, unroll=4)


def _na_bias_table(rpb):
    n_rel = 2 * NA_KW - 1
    qc = np.arange(GRID_W)[:, None]
    kc = np.arange(GRID_W)[None, :]
    start = np.clip(qc - NA_KW // 2, 0, GRID_W - NA_KW)
    valid = (kc >= start) & (kc < start + NA_KW)
    onehot = ((np.arange(n_rel)[:, None, None] == (kc - qc + NA_KW - 1)[None]) & valid[None]).astype(np.float32)
    exp = jnp.einsum("hrc,cqk->hqrk", rpb, jnp.asarray(onehot), precision=lax.Precision.HIGHEST)
    exp = exp + jnp.asarray(np.where(valid, 0.0, NEG).astype(np.float32))[None, :, None, :]
    tabs = [exp[:, :, rel0:rel0 + NA_KH, :].reshape(NA_H, GRID_W, NA_KH * GRID_W) for rel0 in range(NA_KH)]
    return jnp.stack(tabs, axis=1)


def _na_lat(qkv, bias_tab, o_buf):
    ctx_blk = T_LAT // LC
    return pl.pallas_call(
        _na_kernel,
        out_shape=jax.ShapeDtypeStruct(o_buf.shape, o_buf.dtype),
        grid=(B, NA_H),
        in_specs=[pl.BlockSpec((S, HD), lambda b, h: (b, COL_NA_Q + h)),
                  pl.BlockSpec((S, HD), lambda b, h: (b, COL_NA_K + h)),
                  pl.BlockSpec((S, HD), lambda b, h: (b, COL_NA_V + h)),
                  pl.BlockSpec((LC, HD), lambda b, h: (ctx_blk + b, COL_NA_K + h)),
                  pl.BlockSpec((LC, HD), lambda b, h: (ctx_blk + b, COL_NA_V + h)),
                  pl.BlockSpec((1, NA_KH, GRID_W, NA_KH * GRID_W), lambda b, h: (h, 0, 0, 0)),
                  pl.BlockSpec(memory_space=pl.ANY)],
        out_specs=pl.BlockSpec((S, HD), lambda b, h: (b, h)),
        input_output_aliases={6: 0},
        compiler_params=pltpu.CompilerParams(dimension_semantics=("parallel", "parallel")),
        name="na_lat",
    )(qkv, qkv, qkv, qkv, qkv, bias_tab, o_buf)


def _gate_kernel(a_ref, w_ref, b_ref, o_ref):
    g = jnp.dot(a_ref[...], w_ref[...].astype(BF16), preferred_element_type=F32) + b_ref[...]
    g = GATE_SOFTCAP * jnp.tanh(g / GATE_SOFTCAP)
    lane = lax.broadcasted_iota(jnp.int32, g.shape, 1)
    is_forget = (lane // ML_H) % 2 == 1
    log_sig = jnp.minimum(g, 0.0) - jnp.log(1.0 + jnp.exp(-jnp.abs(g)))
    o_ref[...] = jnp.where(is_forget, log_sig, g)


def _mlstm_gates(h, wg, bg):
    tm = TM
    return pl.pallas_call(
        _gate_kernel,
        out_shape=jax.ShapeDtypeStruct((T_ALL, LANES), F32),
        grid=(T_ALL // tm,),
        in_specs=[pl.BlockSpec((tm, D), lambda i: (i, 0)),
                  pl.BlockSpec((D, LANES), lambda i: (0, 0)),
                  pl.BlockSpec((1, LANES), lambda i: (0, 0))],
        out_specs=pl.BlockSpec((tm, LANES), lambda i: (i, 0)),
        compiler_params=pltpu.CompilerParams(dimension_semantics=("parallel",)),
        name="mlstm_gates",
    )(h, wg, bg)


ML_SCALE = ML_DK ** -0.5
_TN = (((0,), (0,)), ((), ()))


def _mlstm_chunk(q, k, v, ig, fg, c_st, n_st, m_st, rev):
    L = q.shape[0]
    ti = lax.broadcasted_iota(jnp.int32, (L, L), 0)
    si = lax.broadcasted_iota(jnp.int32, (L, L), 1)
    eye = ti == si
    before_col = (si >= ti) if rev else (si <= ti)
    before_row = (ti >= si) if rev else (ti <= si)
    f_col = jnp.sum(jnp.where(eye, fg, 0.0), axis=1, keepdims=True)
    i_col = jnp.sum(jnp.where(eye, ig, 0.0), axis=1, keepdims=True)
    b_col = jnp.sum(jnp.where(before_col, fg, 0.0), axis=1, keepdims=True)
    b_row = jnp.sum(jnp.where(before_row, f_col, 0.0), axis=0, keepdims=True)
    total = jnp.sum(fg, axis=1, keepdims=True)
    g_col = total - b_col + i_col
    m_new = jnp.maximum(total + m_st, jnp.max(g_col, axis=0, keepdims=True))
    decay = jnp.exp(total + m_st - m_new)
    wk = jnp.exp(g_col - m_new)
    kw = k.astype(F32) * wk
    c_new = decay * c_st + lax.dot_general(kw.astype(BF16), v, _TN, preferred_element_type=F32)
    n_new = decay * n_st + jnp.sum(kw, axis=0, keepdims=True)

    dmat = jnp.where(before_col, b_col - b_row + ig, NEG)
    inter = b_col + m_st
    m_t = jnp.maximum(inter, jnp.max(dmat, axis=1, keepdims=True))
    a = jnp.exp(inter - m_t)
    qk = lax.dot_general(q, k, _NT, preferred_element_type=F32) * ML_SCALE
    smat = qk * jnp.exp(dmat - m_t)
    num = (a * (jnp.dot(q, c_st.astype(BF16), preferred_element_type=F32) * ML_SCALE)
           + jnp.dot(smat.astype(BF16), v, preferred_element_type=F32))
    den = (a * (jnp.sum(q.astype(F32) * n_st, axis=1, keepdims=True) * ML_SCALE)
           + jnp.sum(smat, axis=1, keepdims=True))
    h = num / jnp.maximum(jnp.abs(den), jnp.exp(-m_t))
    return h, c_new, n_new, m_new


def _mlstm_kernel(qf, kf, vf, gf, qb, kb, vb, gb, hf_ref, hb_ref, cf, nf, mf, cb, nb, mb):
    @pl.when(pl.program_id(1) == 0)
    def _():
        for r in (cf, nf, mf, cb, nb, mb):
            r[...] = jnp.zeros(r.shape, r.dtype)

    gfv = gf[0]
    h, c_new, n_new, m_new = _mlstm_chunk(qf[...], kf[...], vf[...], gfv[0:1], gfv[1:2],
                                          cf[...], nf[...], mf[...], False)
    hf_ref[...] = h
    cf[...] = c_new
    nf[...] = n_new
    mf[...] = m_new
    gbv = gb[0]
    h, c_new, n_new, m_new = _mlstm_chunk(qb[...], kb[...], vb[...], gbv[2:3], gbv[3:4],
                                          cb[...], nb[...], mb[...], True)
    hb_ref[...] = h
    cb[...] = c_new
    nb[...] = n_new
    mb[...] = m_new


def _mlstm_scan(proj, gates):
    L = ML_CHUNK
    n_lat = S // L
    assert LC == L
    ctx_blk = T_LAT // L
    qcol, kcol, vcol = 0, ML_H, (2 * ML_H * ML_DK) // ML_DV

    def fwd(bh, c):
        b = bh // ML_H
        return jnp.where(c == 0, ctx_blk + b, b * n_lat + c - 1)

    def bwd(bh, c):
        b = bh // ML_H
        return jnp.where(c == 0, ctx_blk + b, b * n_lat + n_lat - c)

    def specs(blk):
        return [pl.BlockSpec((L, ML_DK), lambda bh, c: (blk(bh, c), qcol + bh % ML_H)),
                pl.BlockSpec((L, ML_DK), lambda bh, c: (blk(bh, c), kcol + bh % ML_H)),
                pl.BlockSpec((L, ML_DV), lambda bh, c: (blk(bh, c), vcol + bh % ML_H)),
                pl.BlockSpec((1, 4, L), lambda bh, c: (bh % ML_H, 0, blk(bh, c)))]

    out_sds = jax.ShapeDtypeStruct((T_ALL, ML_H * ML_DV), F32)
    return pl.pallas_call(
        _mlstm_kernel,
        out_shape=[out_sds, out_sds],
        grid=(B * ML_H, 1 + n_lat),
        in_specs=specs(fwd) + specs(bwd),
        out_specs=[pl.BlockSpec((L, ML_DV), lambda bh, c: (fwd(bh, c), bh % ML_H)),
                   pl.BlockSpec((L, ML_DV), lambda bh, c: (bwd(bh, c), bh % ML_H))],
        scratch_shapes=[pltpu.VMEM((ML_DK, ML_DV), F32), pltpu.VMEM((1, ML_DK), F32), pltpu.VMEM((1, 1), F32),
                        pltpu.VMEM((ML_DK, ML_DV), F32), pltpu.VMEM((1, ML_DK), F32), pltpu.VMEM((1, 1), F32)],
        compiler_params=pltpu.CompilerParams(dimension_semantics=("parallel", "arbitrary")),
        name="mlstm_scan",
    )(proj, proj, proj, gates, proj, proj, proj, gates)


def _readout_kernel(hf_ref, hb_ref, o_ref, gain_ref, out_ref):
    hs = hf_ref[...] + hb_ref[...]
    o = o_ref[...].astype(F32)
    for hh in range(ML_H):
        sl = slice(hh * ML_DV, (hh + 1) * ML_DV)
        x = hs[:, sl]
        ms = jnp.mean(x * x, axis=-1, keepdims=True)
        y = x * lax.rsqrt(ms + NORM_EPS) * gain_ref[:, sl]
        og = o[:, sl]
        out_ref[:, sl] = (y * (1.0 / (1.0 + jnp.exp(-og)))).astype(out_ref.dtype)


def _mlstm_readout(hf, hb, proj, gain, n_rows):
    tm = TNORM
    wide = ML_H * ML_DV
    ocol = (2 * ML_H * ML_DK + ML_H * ML_DV) // wide
    return pl.pallas_call(
        _readout_kernel,
        out_shape=jax.ShapeDtypeStruct((n_rows, wide), BF16),
        grid=(n_rows // tm,),
        in_specs=[pl.BlockSpec((tm, wide), lambda i: (i, 0)),
                  pl.BlockSpec((tm, wide), lambda i: (i, 0)),
                  pl.BlockSpec((tm, wide), lambda i: (i, ocol)),
                  pl.BlockSpec((1, wide), lambda i: (0, 0))],
        out_specs=pl.BlockSpec((tm, wide), lambda i: (i, 0)),
        compiler_params=pltpu.CompilerParams(dimension_semantics=("parallel",)),
        name="mlstm_readout",
    )(hf, hb, proj, gain)


def _dispatch_kernel(nu_ref, order_ref, nv_ref, shift_ref, h_hbm, o_ref, xf_ref, sem):
    i = pl.program_id(0)
    bm = o_ref.shape[0]
    n_used = nu_ref[0]

    def row_copy(blk, r):
        slot = blk % 2
        src = jnp.minimum(r + shift_ref[blk], order_ref.shape[0] - 1)
        tok = jnp.where(r < nv_ref[blk], order_ref[src] // TOP_K, 0)
        return pltpu.make_async_copy(h_hbm.at[pl.ds(tok, 1)], xf_ref.at[slot, pl.ds(r, 1)], sem.at[slot])

    def start_block(blk):
        def body(r, carry):
            row_copy(blk, r).start()
            return carry
        lax.fori_loop(0, bm, body, 0, unroll=8)

    @pl.when(i == 0)
    def _():
        start_block(0)

    @pl.when(i + 1 < n_used)
    def _():
        start_block(i + 1)

    @pl.when(i < n_used)
    def _():
        def body(r, carry):
            row_copy(i, r).wait()
            return carry
        lax.fori_loop(0, bm, body, 0, unroll=8)
        o_ref[...] = xf_ref[i % 2].astype(BF16)

    @pl.when(i >= n_used)
    def _():
        o_ref[...] = jnp.zeros(o_ref.shape, o_ref.dtype)


def _moe_dispatch(h, order, block_valid, block_shift, n_used):
    bm = MOE_BM
    n_blocks = block_valid.shape[0]
    grid_spec = pltpu.PrefetchScalarGridSpec(
        num_scalar_prefetch=4,
        grid=(n_blocks,),
        in_specs=[pl.BlockSpec(memory_space=pl.ANY)],
        out_specs=pl.BlockSpec((bm, D), lambda i, nu, od, nv, sh: (i, 0)),
        scratch_shapes=[pltpu.VMEM((2, bm, D), F32), pltpu.SemaphoreType.DMA((2,))],
    )
    return pl.pallas_call(
        _dispatch_kernel,
        out_shape=jax.ShapeDtypeStruct((n_blocks * bm, D), BF16),
        grid_spec=grid_spec,
        compiler_params=pltpu.CompilerParams(dimension_semantics=("arbitrary",)),
        name="moe_dispatch",
    )(n_used, order, block_valid, block_shift, h)


def _moe_kernel(be_ref, nu_ref, nv_ref, x_ref, w1_ref, b1_ref, w2_ref, b2_ref, sel_ref, y_ref):
    i = pl.program_id(0)
    f = pl.program_id(1)
    n_blocks = pl.num_programs(0)
    bm = x_ref.shape[0]
    n_used = nu_ref[0]

    @pl.when(jnp.logical_and(f == 0, i < n_used))
    def _():
        y_ref[...] = jnp.broadcast_to(b2_ref[0], y_ref.shape)

    def compute(m):
        h = jnp.dot(x_ref[0:m, :], w1_ref[0].astype(BF16), preferred_element_type=F32) + b1_ref[0]
        glu = jnp.minimum(h, SWIGLU_LIMIT)
        glu = glu * (1.0 / (1.0 + jnp.exp(-SWIGLU_ALPHA * glu)))
        lin = jnp.clip(h, -SWIGLU_LIMIT, SWIGLU_LIMIT) + 1.0
        prod = glu * pltpu.roll(lin, h.shape[1] - 1, 1)
        act = jnp.dot(prod.astype(BF16), sel_ref[...], preferred_element_type=F32)
        y_ref[0:m, :] += jnp.dot(act.astype(BF16), w2_ref[0].astype(BF16), preferred_element_type=F32)

    n_valid = jnp.where(i < n_used, nv_ref[jnp.minimum(i, n_blocks - 1)], 0)
    for k in range(1, bm // MOE_SUB + 1):
        @pl.when(jnp.logical_and(n_valid > (k - 1) * MOE_SUB, n_valid <= k * MOE_SUB))
        def _(k=k):
            compute(k * MOE_SUB)

    @pl.when(jnp.logical_and(i >= n_used, f == 0))
    def _():
        y_ref[...] = jnp.zeros(y_ref.shape, y_ref.dtype)


def _moe_blocks_max(n_tok):
    m = n_tok * TOP_K
    return -(-(m + N_EXP * (MOE_BM - 1)) // MOE_BM)


def _moe_experts(rows, block_expert, block_valid, n_used, layer, w1, b1, w2, b2, sel):
    bm, fh = MOE_BM, MOE_FH
    n_blocks = block_expert.shape[0]
    nf = D_EXP // fh
    e0 = layer * N_EXP

    def last(i, nu):
        return jnp.minimum(i, nu[0] - 1)

    grid_spec = pltpu.PrefetchScalarGridSpec(
        num_scalar_prefetch=3,
        grid=(n_blocks, nf),
        in_specs=[pl.BlockSpec((bm, D), lambda i, f, be, nu, nv: (last(i, nu), 0)),
                  pl.BlockSpec((1, D, 2 * fh), lambda i, f, be, nu, nv: (e0 + be[last(i, nu)], 0, f)),
                  pl.BlockSpec((1, 1, 2 * fh), lambda i, f, be, nu, nv: (be[last(i, nu)], 0, f)),
                  pl.BlockSpec((1, fh, D), lambda i, f, be, nu, nv: (e0 + be[last(i, nu)], f, 0)),
                  pl.BlockSpec((1, 1, D), lambda i, f, be, nu, nv: (be[last(i, nu)], 0, 0)),
                  pl.BlockSpec((2 * fh, fh), lambda i, f, be, nu, nv: (0, 0))],
        out_specs=pl.BlockSpec((bm, D), lambda i, f, be, nu, nv: (i, 0)),
    )
    return pl.pallas_call(
        _moe_kernel,
        out_shape=jax.ShapeDtypeStruct((n_blocks * bm, D), F32),
        grid_spec=grid_spec,
        compiler_params=pltpu.CompilerParams(dimension_semantics=("arbitrary", "arbitrary")),
        name="moe_experts",
    )(block_expert, n_used, block_valid, rows, w1, b1, w2, b2, sel)


def _combine_kernel(dest_ref, x_ref, y_hbm, p_ref, gate_ref, o_ref, buf, sem):
    i = pl.program_id(0)
    tm = x_ref.shape[0]

    def row_copy(kk, r):
        d = dest_ref[(i * tm + r) * TOP_K + kk]
        return pltpu.make_async_copy(y_hbm.at[pl.ds(d, 1)], buf.at[kk, pl.ds(r, 1)], sem.at[kk])

    for kk in range(TOP_K):
        def start(r, carry, kk=kk):
            row_copy(kk, r).start()
            return carry
        lax.fori_loop(0, tm, start, 0, unroll=8)

    p = p_ref[...]
    acc = None
    for kk in range(TOP_K):
        def wait(r, carry, kk=kk):
            row_copy(kk, r).wait()
            return carry
        lax.fori_loop(0, tm, wait, 0, unroll=8)
        term = p[:, kk:kk + 1] * buf[kk]
        acc = term if acc is None else acc + term
    o_ref[...] = x_ref[...] + gate_ref[0] * acc


def _moe_combine(xs, y_rows, dest, probs, mod, gate_chunk, n_rows):
    tm = 256
    grid_spec = pltpu.PrefetchScalarGridSpec(
        num_scalar_prefetch=1,
        grid=(n_rows // tm,),
        in_specs=[pl.BlockSpec((tm, D), lambda i, dst: (i, 0)),
                  pl.BlockSpec(memory_space=pl.ANY),
                  pl.BlockSpec((tm, LANES), lambda i, dst: (i, 0)),
                  pl.BlockSpec((1, 1, D), lambda i, dst: (_mod_row(i, tm), 0, gate_chunk))],
        out_specs=pl.BlockSpec((tm, D), lambda i, dst: (i, 0)),
        scratch_shapes=[pltpu.VMEM((TOP_K, tm, D), F32), pltpu.SemaphoreType.DMA((TOP_K,))],
    )
    return pl.pallas_call(
        _combine_kernel,
        out_shape=jax.ShapeDtypeStruct((n_rows, D), F32),
        grid_spec=grid_spec,
        compiler_params=pltpu.CompilerParams(dimension_semantics=("arbitrary",)),
        name="moe_combine",
    )(dest, xs, y_rows, probs, mod)


def _moe_route(top_e, n_tok):
    m = n_tok * TOP_K
    bm = MOE_BM
    n_blocks = _moe_blocks_max(n_tok)
    e_flat = top_e.reshape(m)
    onehot = (e_flat[:, None] == jnp.arange(N_EXP, dtype=jnp.int32)[None, :]).astype(jnp.int32)
    csum = jnp.cumsum(onehot, axis=0)
    counts = csum[-1]
    rank = jnp.take_along_axis(csum, e_flat[:, None], axis=1)[:, 0] - 1
    padded = (counts + bm - 1) // bm * bm
    pad_end = jnp.cumsum(padded)
    pad_start = pad_end - padded
    dest = pad_start[e_flat] + rank
    n_used = (pad_end[-1] // bm).astype(jnp.int32).reshape(1)
    starts = jnp.arange(n_blocks, dtype=jnp.int32) * bm
    block_expert = jnp.minimum(jnp.sum((pad_end[None, :] <= starts[:, None]).astype(jnp.int32), axis=1),
                               N_EXP - 1).astype(jnp.int32)
    order = jnp.argsort(e_flat, stable=True).astype(jnp.int32)
    sort_start = jnp.cumsum(counts) - counts
    block_valid = jnp.clip((pad_start + counts)[block_expert] - starts, 0, bm).astype(jnp.int32)
    block_shift = jnp.maximum(sort_start[block_expert] + starts - pad_start[block_expert], 0).astype(jnp.int32)
    return dest, order, block_expert, block_valid, block_shift, n_used


def _moe_layer(xs, n_rows, g, mod, router_w, router_b, layer, w1_all, b1, w2_all, b2, sel, name):
    rw = jnp.pad(router_w, ((0, 0), (0, LANES - N_EXP)))
    rb = jnp.pad(router_b, (0, LANES - N_EXP), constant_values=NEG).reshape(1, LANES)
    h, top_e, probs = _norm_mod(xs, g, mod, 3, 4, n_rows=n_rows, router_w=rw, router_b=rb, name=name + "_norm")
    dest, order, block_expert, block_valid, block_shift, n_used = _moe_route(top_e[:, :TOP_K], n_rows)
    rows = _moe_dispatch(h, order, block_valid, block_shift, n_used)
    y_rows = _moe_experts(rows, block_expert, block_valid, n_used, layer, w1_all,
                          b1.reshape(N_EXP, 1, 2 * D_EXP), w2_all, b2.reshape(N_EXP, 1, D), sel)
    return _moe_combine(xs, y_rows, dest, probs, mod, 5, n_rows)


def _select_matrix():
    r = np.arange(2 * MOE_FH)[:, None]
    c = np.arange(MOE_FH)[None, :]
    return jnp.asarray(r == 2 * c, BF16)


def kernel(x, c, ctx, c_ctx, ada_w, ada_b, norm_g, attn_w_in, attn_w_out, attn_qk_gain, na_rpb, mlstm_w_in,
           mlstm_gate_bias, mlstm_head_gain, mlstm_w_out, router_w, router_b, expert_w1, expert_b1,
           expert_w2, expert_b2):
    xs = jnp.concatenate([x.reshape(T_LAT, D), ctx.reshape(T_CTX, D)], axis=0)
    cc = jnp.zeros((8, D), F32).at[:B].set(c).at[MOD_CTX_ROW].set(c_ctx)
    sel = _select_matrix()

    n_layers = ada_w.shape[0]
    ada_w2 = ada_w.reshape(n_layers * D, 6 * D)
    w1_all = expert_w1.reshape(n_layers * N_EXP, D, 2 * D_EXP)
    w2_all = expert_w2.reshape(n_layers * N_EXP, D_EXP, D)

    def ada(layer):
        m = _matmul(cc, ada_w2, 6 * D, out_dtype=F32, tm=8, w_row_block=layer, silu=True,
                    bias=ada_b[layer].reshape(1, 6 * D), name="adaln")
        return m.reshape(8, 1, 6 * D)

    mod = ada(0)
    h = _norm_mod(xs, norm_g[0, 0].reshape(1, D), mod, 0, 1, n_rows=T_ALL, name="l0_norm1")
    qkv = _matmul(h, attn_w_in[0], ATTN_IN, out_dtype=BF16, name="attn_in")
    cos_rep, sin_signed = _rope_tables()
    qkv = _qk_prep(qkv, attn_qk_gain[0].reshape(4, 1, HD), cos_rep, sin_signed)
    o_buf = _na_lat(qkv, _na_bias_table(na_rpb[0]), jnp.zeros((T_ALL, D), BF16))
    o_buf = _gqa_lat(qkv, o_buf)
    o_buf = _ctx_attn(qkv, o_buf)
    xs = _matmul(o_buf, attn_w_out[0], D, out_dtype=F32, res=xs, mod=mod, gate_chunk=2, name="attn_out")
    xs = _moe_layer(xs, T_ALL, norm_g[0, 1].reshape(1, D), mod, router_w[0], router_b[0],
                    0, w1_all, expert_b1[0], w2_all, expert_b2[0], sel, "l0_moe")

    mod = ada(1)
    h = _norm_mod(xs, norm_g[1, 0].reshape(1, D), mod, 0, 1, n_rows=T_ALL, name="l1_norm1")
    proj = _matmul(h, mlstm_w_in[0], ML_MAIN, out_dtype=BF16, name="mlstm_in")
    wg = jnp.pad(mlstm_w_in[0][:, ML_MAIN:], ((0, 0), (0, LANES - 4 * ML_H)))
    bg = jnp.pad(mlstm_gate_bias[0].reshape(-1), (0, LANES - 4 * ML_H)).reshape(1, LANES)
    g = _mlstm_gates(h, wg, bg)
    gates = g[:, :4 * ML_H].T.reshape(4, ML_H, T_ALL).transpose(1, 0, 2)
    hf, hb = _mlstm_scan(proj, gates)
    hn = _mlstm_readout(hf, hb, proj, mlstm_head_gain[0].reshape(1, ML_H * ML_DV), T_LAT)
    xs = _matmul(hn, mlstm_w_out[0], D, out_dtype=F32, m_rows=T_LAT, res=xs, mod=mod, gate_chunk=2,
                 name="mlstm_out")
    out = _moe_layer(xs, T_LAT, norm_g[1, 1].reshape(1, D), mod, router_w[1], router_b[1],
                     1, w1_all, expert_b1[1], w2_all, expert_b2[1], sel, "l1_moe")
    return out.reshape(B, S, D)
```

```python
import functools

import jax
import jax.numpy as jnp
import numpy as np
from jax import lax
from jax.experimental import pallas as pl
from jax.experimental.pallas import tpu as pltpu

F32 = jnp.float32
BF16 = jnp.bfloat16

D = 2048
B = 4
S = 2048
LC = 256
GRID_W = 64
ROWS = S // GRID_W
HD = 128
NA_H = 8
GQA_H = 8
GQA_KV = 2
NA_KH = 8
NA_KW = 16
ROPE_THETA = 10000.0
ML_H = 8
ML_DK = 128
ML_DV = 256
GATE_SOFTCAP = 15.0
N_EXP = 32
TOP_K = 4
D_EXP = D
SWIGLU_ALPHA = 1.702
SWIGLU_LIMIT = 7.0
NORM_EPS = 1e-6
ATTN_IN = 3 * NA_H * HD + GQA_H * HD + 2 * GQA_KV * HD
ML_MAIN = 2 * ML_H * ML_DK + 2 * ML_H * ML_DV

T_LAT = B * S
T_CTX = B * LC
T_ALL = T_LAT + T_CTX
MOD_CTX_ROW = B

LANES = 128

TM = 1024
TN = 512
TNORM = 512
ML_CHUNK = 256
MOE_BM = 512
MOE_FH = 512
MOE_NF = D_EXP // MOE_FH
MOE_SUB = 128
NEG = -1e30


def _mod_row(i, tm):
    n_lat = T_LAT // tm
    per_b = S // tm
    return jnp.where(i < n_lat, i // per_b, MOD_CTX_ROW)


def _mm_kernel(*refs, silu, has_bias, has_res):
    a_ref, w_ref = refs[0], refs[1]
    pos = 2
    bias_ref = res_ref = gate_ref = None
    if has_bias:
        bias_ref = refs[pos]
        pos += 1
    if has_res:
        res_ref, gate_ref = refs[pos], refs[pos + 1]
        pos += 2
    o_ref = refs[pos]
    a = a_ref[...]
    if silu:
        a = a.astype(F32)
        a = a * (1.0 / (1.0 + jnp.exp(-a)))
    acc = jnp.dot(a.astype(BF16), w_ref[...].astype(BF16), preferred_element_type=F32)
    if has_bias:
        acc = acc + bias_ref[...]
    if has_res:
        acc = res_ref[...] + gate_ref[0] * acc
    o_ref[...] = acc.astype(o_ref.dtype)


def _matmul(a, w, n_out, *, out_dtype, tm=TM, tn=TN, m_rows=None, w_row_block=0, silu=False, bias=None,
            res=None, mod=None, gate_chunk=None, name="mm"):
    m_rows = a.shape[0] if m_rows is None else m_rows
    k = a.shape[1]
    grid = (m_rows // tm, n_out // tn)
    in_specs = [pl.BlockSpec((tm, k), lambda i, j: (i, 0)),
                pl.BlockSpec((k, tn), lambda i, j: (w_row_block, j))]
    args = [a, w]
    if bias is not None:
        in_specs.append(pl.BlockSpec((1, tn), lambda i, j: (0, j)))
        args.append(bias)
    if res is not None:
        per = D // tn
        in_specs.append(pl.BlockSpec((tm, tn), lambda i, j: (i, j)))
        in_specs.append(pl.BlockSpec((1, 1, tn), lambda i, j: (_mod_row(i, tm), 0, gate_chunk * per + j)))
        args += [res, mod]
    return pl.pallas_call(
        functools.partial(_mm_kernel, silu=silu, has_bias=bias is not None, has_res=res is not None),
        out_shape=jax.ShapeDtypeStruct((m_rows, n_out), out_dtype),
        grid=grid, in_specs=in_specs,
        out_specs=pl.BlockSpec((tm, tn), lambda i, j: (i, j)),
        compiler_params=pltpu.CompilerParams(dimension_semantics=("parallel", "parallel")),
        name=name,
    )(*args)


def _norm_kernel(*refs, router):
    x_ref, g_ref, shift_ref, scale_ref = refs[:4]
    x = x_ref[...]
    ms = jnp.mean(x * x, axis=-1, keepdims=True)
    y = x * lax.rsqrt(ms + NORM_EPS) * g_ref[...]
    h = y * (1.0 + scale_ref[0]) + shift_ref[0]
    hb = h.astype(BF16)
    if not router:
        refs[4][...] = hb
        return
    rw_ref, rb_ref, h_ref, e_ref, p_ref = refs[4:]
    h_ref[...] = h
    logits = jnp.dot(hb, rw_ref[...].astype(BF16), preferred_element_type=F32) + rb_ref[...]
    lane = lax.broadcasted_iota(jnp.int32, logits.shape, 1)
    e_out = jnp.zeros(logits.shape, jnp.int32)
    v_out = jnp.full(logits.shape, NEG, F32)
    work = logits
    for kk in range(TOP_K):
        mx = jnp.max(work, axis=-1, keepdims=True)
        idx = jnp.min(jnp.where(work == mx, lane, LANES), axis=-1, keepdims=True)
        e_out = jnp.where(lane == kk, idx, e_out)
        v_out = jnp.where(lane == kk, mx, v_out)
        work = jnp.where(lane == idx, -jnp.inf, work)
    top0 = jnp.max(v_out, axis=-1, keepdims=True)
    pe = jnp.exp(v_out - top0)
    e_ref[...] = e_out
    p_ref[...] = pe / jnp.sum(pe, axis=-1, keepdims=True)


def _norm_mod(xs, g, mod, shift_chunk, scale_chunk, *, n_rows, router_w=None, router_b=None, name="norm"):
    tm = TNORM
    in_specs = [pl.BlockSpec((tm, D), lambda i: (i, 0)),
                pl.BlockSpec((1, D), lambda i: (0, 0)),
                pl.BlockSpec((1, 1, D), lambda i: (_mod_row(i, tm), 0, shift_chunk)),
                pl.BlockSpec((1, 1, D), lambda i: (_mod_row(i, tm), 0, scale_chunk))]
    args = [xs, g, mod, mod]
    router = router_w is not None
    out_shape = [jax.ShapeDtypeStruct((n_rows, D), F32 if router else BF16)]
    out_specs = [pl.BlockSpec((tm, D), lambda i: (i, 0))]
    if router:
        in_specs += [pl.BlockSpec((D, LANES), lambda i: (0, 0)),
                     pl.BlockSpec((1, LANES), lambda i: (0, 0))]
        args += [router_w, router_b]
        out_shape += [jax.ShapeDtypeStruct((n_rows, LANES), jnp.int32),
                      jax.ShapeDtypeStruct((n_rows, LANES), F32)]
        out_specs += [pl.BlockSpec((tm, LANES), lambda i: (i, 0))] * 2
    out = pl.pallas_call(
        functools.partial(_norm_kernel, router=router),
        out_shape=out_shape, grid=(n_rows // tm,), in_specs=in_specs, out_specs=out_specs,
        compiler_params=pltpu.CompilerParams(dimension_semantics=("parallel",)),
        name=name,
    )(*args)
    return out if router else out[0]


QK_PAIR = 2
N_QK_SLOTS = (2 * NA_H + GQA_H + GQA_KV) // QK_PAIR
QK_NA_SLOTS = 2 * NA_H // QK_PAIR


def _qk_col(j):
    return jnp.where(j < QK_NA_SLOTS, j, j + NA_H // QK_PAIR)


def _qk_gain_row(j):
    half = NA_H // QK_PAIR
    return jnp.where(j < half, 0, jnp.where(j < 2 * half, 1, jnp.where(j < 3 * half, 2, 3)))


def _qk_kernel(x_ref, gain_ref, cos_ref, sin_ref, o_ref, *, tm):
    i = pl.program_id(0)
    j = pl.program_id(1)
    do_rope = jnp.logical_and(j >= QK_NA_SLOTS, i < T_LAT // tm)
    for hh in range(QK_PAIR):
        sl = slice(hh * HD, (hh + 1) * HD)
        y = x_ref[:, sl].astype(F32)
        ms = jnp.mean(y * y, axis=-1, keepdims=True)
        yn = y * lax.rsqrt(ms + NORM_EPS) * gain_ref[0]
        lane = lax.broadcasted_iota(jnp.int32, yn.shape, 1)
        nxt = pltpu.roll(yn, LANES - 1, 1)
        prv = pltpu.roll(yn, 1, 1)
        partner = jnp.where(lane % 2 == 0, nxt, prv)
        yr = yn * cos_ref[...] + partner * sin_ref[...]
        o_ref[:, sl] = jnp.where(do_rope, yr, yn).astype(o_ref.dtype)


def _qk_prep(qkv, gain, cos_rep, sin_signed):
    tm = TM
    per_b = S // tm
    wide = QK_PAIR * HD
    return pl.pallas_call(
        functools.partial(_qk_kernel, tm=tm),
        out_shape=jax.ShapeDtypeStruct(qkv.shape, qkv.dtype),
        grid=(T_ALL // tm, N_QK_SLOTS),
        in_specs=[pl.BlockSpec((tm, wide), lambda i, j: (i, _qk_col(j))),
                  pl.BlockSpec((1, 1, HD), lambda i, j: (_qk_gain_row(j), 0, 0)),
                  pl.BlockSpec((tm, HD), lambda i, j: (i % per_b, 0)),
                  pl.BlockSpec((tm, HD), lambda i, j: (i % per_b, 0))],
        out_specs=pl.BlockSpec((tm, wide), lambda i, j: (i, _qk_col(j))),
        input_output_aliases={0: 0},
        compiler_params=pltpu.CompilerParams(dimension_semantics=("parallel", "parallel")),
        name="qk_prep",
    )(qkv, gain, cos_rep, sin_signed)


def _rope_tables():
    t = np.arange(S)
    row = (t // GRID_W).astype(np.float32)
    col = (t % GRID_W).astype(np.float32)
    axis_dim = HD // 2
    inv_freq = jnp.asarray(ROPE_THETA, F32) ** (-jnp.arange(0, axis_dim, 2, dtype=F32) / axis_dim)
    ang = jnp.concatenate([jnp.asarray(row)[:, None] * inv_freq, jnp.asarray(col)[:, None] * inv_freq], axis=-1)
    cos, sin = jnp.cos(ang), jnp.sin(ang)
    cos_rep = jnp.repeat(cos, 2, axis=-1)
    sin_signed = jnp.stack([-sin, sin], axis=-1).reshape(S, HD)
    return cos_rep, sin_signed


COL_NA_Q, COL_NA_K, COL_NA_V = 0, NA_H, 2 * NA_H
COL_G_Q, COL_G_K, COL_G_V = 3 * NA_H, 3 * NA_H + GQA_H, 3 * NA_H + GQA_H + GQA_KV
ATT_SCALE = HD ** -0.5
_NT = (((1,), (1,)), ((), ()))


def _attn_kernel(*refs, two):
    if two:
        q_ref, k1_ref, v1_ref, k2_ref, v2_ref, _, o_ref = refs
    else:
        q_ref, k1_ref, v1_ref, _, o_ref = refs
    q = q_ref[...]
    s1 = lax.dot_general(q, k1_ref[...], _NT, preferred_element_type=F32) * ATT_SCALE
    m = jnp.max(s1, axis=-1, keepdims=True)
    if two:
        s2 = lax.dot_general(q, k2_ref[...], _NT, preferred_element_type=F32) * ATT_SCALE
        m = jnp.maximum(m, jnp.max(s2, axis=-1, keepdims=True))
    p1 = jnp.exp(s1 - m)
    l = jnp.sum(p1, axis=-1, keepdims=True)
    o = jnp.dot(p1.astype(BF16), v1_ref[...], preferred_element_type=F32)
    if two:
        p2 = jnp.exp(s2 - m)
        l = l + jnp.sum(p2, axis=-1, keepdims=True)
        o = o + jnp.dot(p2.astype(BF16), v2_ref[...], preferred_element_type=F32)
    o_ref[...] = (o / l).astype(o_ref.dtype)


def _gqa_lat(qkv, o_buf):
    tq = 512
    nq = S // tq
    ctx_blk = T_LAT // LC
    grp = GQA_H // GQA_KV
    return pl.pallas_call(
        functools.partial(_attn_kernel, two=True),
        out_shape=jax.ShapeDtypeStruct(o_buf.shape, o_buf.dtype),
        grid=(B, GQA_H, nq),
        in_specs=[pl.BlockSpec((tq, HD), lambda b, h, t: (b * nq + t, COL_G_Q + h)),
                  pl.BlockSpec((LC, HD), lambda b, h, t: (ctx_blk + b, COL_G_K + h // grp)),
                  pl.BlockSpec((LC, HD), lambda b, h, t: (ctx_blk + b, COL_G_V + h // grp)),
                  pl.BlockSpec((S, HD), lambda b, h, t: (b, COL_G_K + h // grp)),
                  pl.BlockSpec((S, HD), lambda b, h, t: (b, COL_G_V + h // grp)),
                  pl.BlockSpec(memory_space=pl.ANY)],
        out_specs=pl.BlockSpec((tq, HD), lambda b, h, t: (b * nq + t, NA_H + h)),
        input_output_aliases={5: 0},
        compiler_params=pltpu.CompilerParams(dimension_semantics=("parallel", "parallel", "parallel")),
        name="gqa_lat",
    )(qkv, qkv, qkv, qkv, qkv, o_buf)


def _ctx_attn(qkv, o_buf):
    ctx_blk = T_LAT // LC
    grp = GQA_H // GQA_KV

    def qcol(h):
        return jnp.where(h < NA_H, COL_NA_Q + h, COL_G_Q + h - NA_H)

    def kcol(h):
        return jnp.where(h < NA_H, COL_NA_K + h, COL_G_K + (h - NA_H) // grp)

    def vcol(h):
        return jnp.where(h < NA_H, COL_NA_V + h, COL_G_V + (h - NA_H) // grp)

    return pl.pallas_call(
        functools.partial(_attn_kernel, two=False),
        out_shape=jax.ShapeDtypeStruct(o_buf.shape, o_buf.dtype),
        grid=(B, NA_H + GQA_H),
        in_specs=[pl.BlockSpec((LC, HD), lambda b, h: (ctx_blk + b, qcol(h))),
                  pl.BlockSpec((LC, HD), lambda b, h: (ctx_blk + b, kcol(h))),
                  pl.BlockSpec((LC, HD), lambda b, h: (ctx_blk + b, vcol(h))),
                  pl.BlockSpec(memory_space=pl.ANY)],
        out_specs=pl.BlockSpec((LC, HD), lambda b, h: (ctx_blk + b, h)),
        input_output_aliases={3: 0},
        compiler_params=pltpu.CompilerParams(dimension_semantics=("parallel", "parallel")),
        name="ctx_attn",
    )(qkv, qkv, qkv, o_buf)


def _na_kernel(q_ref, k_ref, v_ref, kc_ref, vc_ref, bias_ref, _, o_ref):
    kc = kc_ref[...]
    vc = vc_ref[...]
    n_win = NA_KH * GRID_W

    def body(r, carry):
        r0 = jnp.clip(r - NA_KH // 2, 0, ROWS - NA_KH)
        rel0 = r0 - r + NA_KH - 1
        q = q_ref[pl.ds(pl.multiple_of(r * GRID_W, GRID_W), GRID_W), :]
        kw = k_ref[pl.ds(pl.multiple_of(r0 * GRID_W, GRID_W), n_win), :]
        vw = v_ref[pl.ds(pl.multiple_of(r0 * GRID_W, GRID_W), n_win), :]
        sw = lax.dot_general(q, kw, _NT, preferred_element_type=F32) * ATT_SCALE + bias_ref[0, rel0]
        sc = lax.dot_general(q, kc, _NT, preferred_element_type=F32) * ATT_SCALE
        m = jnp.maximum(jnp.max(sw, axis=-1, keepdims=True), jnp.max(sc, axis=-1, keepdims=True))
        pw = jnp.exp(sw - m)
        pc = jnp.exp(sc - m)
        l = jnp.sum(pw, axis=-1, keepdims=True) + jnp.sum(pc, axis=-1, keepdims=True)
        o = (jnp.dot(pw.astype(BF16), vw, preferred_element_type=F32)
             + jnp.dot(pc.astype(BF16), vc, preferred_element_type=F32))
        o_ref[pl.ds(pl.multiple_of(r * GRID_W, GRID_W), GRID_W), :] = (o / l).astype(o_ref.dtype)
        return carry

    lax.fori_loop(0, ROWS, body, 0, unroll=4)


def _na_bias_table(rpb):
    n_rel = 2 * NA_KW - 1
    qc = np.arange(GRID_W)[:, None]
    kc = np.arange(GRID_W)[None, :]
    start = np.clip(qc - NA_KW // 2, 0, GRID_W - NA_KW)
    valid = (kc >= start) & (kc < start + NA_KW)
    onehot = ((np.arange(n_rel)[:, None, None] == (kc - qc + NA_KW - 1)[None]) & valid[None]).astype(np.float32)
    exp = jnp.einsum("hrc,cqk->hqrk", rpb, jnp.asarray(onehot), precision=lax.Precision.HIGHEST)
    exp = exp + jnp.asarray(np.where(valid, 0.0, NEG).astype(np.float32))[None, :, None, :]
    tabs = [exp[:, :, rel0:rel0 + NA_KH, :].reshape(NA_H, GRID_W, NA_KH * GRID_W) for rel0 in range(NA_KH)]
    return jnp.stack(tabs, axis=1)


def _na_lat(qkv, bias_tab, o_buf):
    ctx_blk = T_LAT // LC
    return pl.pallas_call(
        _na_kernel,
        out_shape=jax.ShapeDtypeStruct(o_buf.shape, o_buf.dtype),
        grid=(B, NA_H),
        in_specs=[pl.BlockSpec((S, HD), lambda b, h: (b, COL_NA_Q + h)),
                  pl.BlockSpec((S, HD), lambda b, h: (b, COL_NA_K + h)),
                  pl.BlockSpec((S, HD), lambda b, h: (b, COL_NA_V + h)),
                  pl.BlockSpec((LC, HD), lambda b, h: (ctx_blk + b, COL_NA_K + h)),
                  pl.BlockSpec((LC, HD), lambda b, h: (ctx_blk + b, COL_NA_V + h)),
                  pl.BlockSpec((1, NA_KH, GRID_W, NA_KH * GRID_W), lambda b, h: (h, 0, 0, 0)),
                  pl.BlockSpec(memory_space=pl.ANY)],
        out_specs=pl.BlockSpec((S, HD), lambda b, h: (b, h)),
        input_output_aliases={6: 0},
        compiler_params=pltpu.CompilerParams(dimension_semantics=("parallel", "parallel")),
        name="na_lat",
    )(qkv, qkv, qkv, qkv, qkv, bias_tab, o_buf)


def _gate_kernel(a_ref, w_ref, b_ref, o_ref):
    g = jnp.dot(a_ref[...], w_ref[...].astype(BF16), preferred_element_type=F32) + b_ref[...]
    g = GATE_SOFTCAP * jnp.tanh(g / GATE_SOFTCAP)
    lane = lax.broadcasted_iota(jnp.int32, g.shape, 1)
    is_forget = (lane // ML_H) % 2 == 1
    log_sig = jnp.minimum(g, 0.0) - jnp.log(1.0 + jnp.exp(-jnp.abs(g)))
    o_ref[...] = jnp.where(is_forget, log_sig, g)


def _mlstm_gates(h, wg, bg):
    tm = TM
    return pl.pallas_call(
        _gate_kernel,
        out_shape=jax.ShapeDtypeStruct((T_ALL, LANES), F32),
        grid=(T_ALL // tm,),
        in_specs=[pl.BlockSpec((tm, D), lambda i: (i, 0)),
                  pl.BlockSpec((D, LANES), lambda i: (0, 0)),
                  pl.BlockSpec((1, LANES), lambda i: (0, 0))],
        out_specs=pl.BlockSpec((tm, LANES), lambda i: (i, 0)),
        compiler_params=pltpu.CompilerParams(dimension_semantics=("parallel",)),
        name="mlstm_gates",
    )(h, wg, bg)


ML_SCALE = ML_DK ** -0.5
_TN = (((0,), (0,)), ((), ()))


def _mlstm_chunk(q, k, v, ig, fg, c_st, n_st, m_st, rev):
    L = q.shape[0]
    ti = lax.broadcasted_iota(jnp.int32, (L, L), 0)
    si = lax.broadcasted_iota(jnp.int32, (L, L), 1)
    eye = ti == si
    before_col = (si >= ti) if rev else (si <= ti)
    before_row = (ti >= si) if rev else (ti <= si)
    f_col = jnp.sum(jnp.where(eye, fg, 0.0), axis=1, keepdims=True)
    i_col = jnp.sum(jnp.where(eye, ig, 0.0), axis=1, keepdims=True)
    b_col = jnp.sum(jnp.where(before_col, fg, 0.0), axis=1, keepdims=True)
    b_row = jnp.sum(jnp.where(before_row, f_col, 0.0), axis=0, keepdims=True)
    total = jnp.sum(fg, axis=1, keepdims=True)
    g_col = total - b_col + i_col
    m_new = jnp.maximum(total + m_st, jnp.max(g_col, axis=0, keepdims=True))
    decay = jnp.exp(total + m_st - m_new)
    wk = jnp.exp(g_col - m_new)
    kw = k.astype(F32) * wk
    c_new = decay * c_st + lax.dot_general(kw.astype(BF16), v, _TN, preferred_element_type=F32)
    n_new = decay * n_st + jnp.sum(kw, axis=0, keepdims=True)

    dmat = jnp.where(before_col, b_col - b_row + ig, NEG)
    inter = b_col + m_st
    m_t = jnp.maximum(inter, jnp.max(dmat, axis=1, keepdims=True))
    a = jnp.exp(inter - m_t)
    qk = lax.dot_general(q, k, _NT, preferred_element_type=F32) * ML_SCALE
    smat = qk * jnp.exp(dmat - m_t)
    num = (a * (jnp.dot(q, c_st.astype(BF16), preferred_element_type=F32) * ML_SCALE)
           + jnp.dot(smat.astype(BF16), v, preferred_element_type=F32))
    den = (a * (jnp.sum(q.astype(F32) * n_st, axis=1, keepdims=True) * ML_SCALE)
           + jnp.sum(smat, axis=1, keepdims=True))
    h = num / jnp.maximum(jnp.abs(den), jnp.exp(-m_t))
    return h, c_new, n_new, m_new


def _mlstm_kernel(qf, kf, vf, gf, qb, kb, vb, gb, hf_ref, hb_ref, cf, nf, mf, cb, nb, mb):
    @pl.when(pl.program_id(1) == 0)
    def _():
        for r in (cf, nf, mf, cb, nb, mb):
            r[...] = jnp.zeros(r.shape, r.dtype)

    gfv = gf[0]
    h, c_new, n_new, m_new = _mlstm_chunk(qf[...], kf[...], vf[...], gfv[0:1], gfv[1:2],
                                          cf[...], nf[...], mf[...], False)
    hf_ref[...] = h
    cf[...] = c_new
    nf[...] = n_new
    mf[...] = m_new
    gbv = gb[0]
    h, c_new, n_new, m_new = _mlstm_chunk(qb[...], kb[...], vb[...], gbv[2:3], gbv[3:4],
                                          cb[...], nb[...], mb[...], True)
    hb_ref[...] = h
    cb[...] = c_new
    nb[...] = n_new
    mb[...] = m_new


def _mlstm_scan(proj, gates):
    L = ML_CHUNK
    n_lat = S // L
    assert LC == L
    ctx_blk = T_LAT // L
    qcol, kcol, vcol = 0, ML_H, (2 * ML_H * ML_DK) // ML_DV

    def fwd(bh, c):
        b = bh // ML_H
        return jnp.where(c == 0, ctx_blk + b, b * n_lat + c - 1)

    def bwd(bh, c):
        b = bh // ML_H
        return jnp.where(c == 0, ctx_blk + b, b * n_lat + n_lat - c)

    def specs(blk):
        return [pl.BlockSpec((L, ML_DK), lambda bh, c: (blk(bh, c), qcol + bh % ML_H)),
                pl.BlockSpec((L, ML_DK), lambda bh, c: (blk(bh, c), kcol + bh % ML_H)),
                pl.BlockSpec((L, ML_DV), lambda bh, c: (blk(bh, c), vcol + bh % ML_H)),
                pl.BlockSpec((1, 4, L), lambda bh, c: (bh % ML_H, 0, blk(bh, c)))]

    out_sds = jax.ShapeDtypeStruct((T_ALL, ML_H * ML_DV), F32)
    return pl.pallas_call(
        _mlstm_kernel,
        out_shape=[out_sds, out_sds],
        grid=(B * ML_H, 1 + n_lat),
        in_specs=specs(fwd) + specs(bwd),
        out_specs=[pl.BlockSpec((L, ML_DV), lambda bh, c: (fwd(bh, c), bh % ML_H)),
                   pl.BlockSpec((L, ML_DV), lambda bh, c: (bwd(bh, c), bh % ML_H))],
        scratch_shapes=[pltpu.VMEM((ML_DK, ML_DV), F32), pltpu.VMEM((1, ML_DK), F32), pltpu.VMEM((1, 1), F32),
                        pltpu.VMEM((ML_DK, ML_DV), F32), pltpu.VMEM((1, ML_DK), F32), pltpu.VMEM((1, 1), F32)],
        compiler_params=pltpu.CompilerParams(dimension_semantics=("parallel", "arbitrary")),
        name="mlstm_scan",
    )(proj, proj, proj, gates, proj, proj, proj, gates)


def _readout_kernel(hf_ref, hb_ref, o_ref, gain_ref, out_ref):
    hs = hf_ref[...] + hb_ref[...]
    o = o_ref[...].astype(F32)
    for hh in range(ML_H):
        sl = slice(hh * ML_DV, (hh + 1) * ML_DV)
        x = hs[:, sl]
        ms = jnp.mean(x * x, axis=-1, keepdims=True)
        y = x * lax.rsqrt(ms + NORM_EPS) * gain_ref[:, sl]
        og = o[:, sl]
        out_ref[:, sl] = (y * (1.0 / (1.0 + jnp.exp(-og)))).astype(out_ref.dtype)


def _mlstm_readout(hf, hb, proj, gain, n_rows):
    tm = TNORM
    wide = ML_H * ML_DV
    ocol = (2 * ML_H * ML_DK + ML_H * ML_DV) // wide
    return pl.pallas_call(
        _readout_kernel,
        out_shape=jax.ShapeDtypeStruct((n_rows, wide), BF16),
        grid=(n_rows // tm,),
        in_specs=[pl.BlockSpec((tm, wide), lambda i: (i, 0)),
                  pl.BlockSpec((tm, wide), lambda i: (i, 0)),
                  pl.BlockSpec((tm, wide), lambda i: (i, ocol)),
                  pl.BlockSpec((1, wide), lambda i: (0, 0))],
        out_specs=pl.BlockSpec((tm, wide), lambda i: (i, 0)),
        compiler_params=pltpu.CompilerParams(dimension_semantics=("parallel",)),
        name="mlstm_readout",
    )(hf, hb, proj, gain)


def _moe_kernel(be_ref, nu_ref, tok_ref, nv_ref, h_hbm, w1_ref, b1_ref, w2_ref, b2_ref, sel_ref, y_ref,
                xf_ref, xb_ref, sem):
    i = pl.program_id(0)
    f = pl.program_id(1)
    n_blocks = pl.num_programs(0)
    nf = pl.num_programs(1)
    bm = xb_ref.shape[0]
    per_tile = bm // MOE_NF
    n_used = nu_ref[0]

    def row_copy(blk, r):
        tok = tok_ref[blk * bm + r]
        return pltpu.make_async_copy(h_hbm.at[pl.ds(tok, 1)], xf_ref.at[pl.ds(r, 1)], sem)

    def wait_block(blk):
        def body(r, carry):
            row_copy(blk, r).wait()
            return carry
        lax.fori_loop(0, bm, body, 0, unroll=8)

    @pl.when(jnp.logical_and(i == 0, f == 0))
    def _():
        def body(r, carry):
            row_copy(0, r).start()
            return carry
        lax.fori_loop(0, bm, body, 0, unroll=8)

    @pl.when(jnp.logical_and(f == 0, i <= n_used))
    def _():
        wait_block(i)
        xb_ref[...] = xf_ref[...].astype(BF16)
        y_ref[...] = jnp.broadcast_to(b2_ref[0], y_ref.shape)

    def compute(m):
        nxt = jnp.minimum(i + 1, n_blocks - 1)
        for r in range(per_tile):
            row_copy(nxt, f * per_tile + r).start(priority=r % 2)
        h = jnp.dot(xb_ref[0:m, :], w1_ref[0].astype(BF16), preferred_element_type=F32) + b1_ref[0]
        glu = jnp.minimum(h, SWIGLU_LIMIT)
        glu = glu * (1.0 / (1.0 + jnp.exp(-SWIGLU_ALPHA * glu)))
        lin = jnp.clip(h, -SWIGLU_LIMIT, SWIGLU_LIMIT) + 1.0
        prod = glu * pltpu.roll(lin, h.shape[1] - 1, 1)
        act = jnp.dot(prod.astype(BF16), sel_ref[...], preferred_element_type=F32)
        y_ref[0:m, :] += jnp.dot(act.astype(BF16), w2_ref[0].astype(BF16), preferred_element_type=F32)

    n_valid = jnp.where(i < n_used, nv_ref[jnp.minimum(i, n_blocks - 1)], 0)
    for k in range(1, bm // MOE_SUB + 1):
        @pl.when(jnp.logical_and(n_valid > (k - 1) * MOE_SUB, n_valid <= k * MOE_SUB))
        def _(k=k):
            compute(k * MOE_SUB)

    @pl.when(jnp.logical_and(jnp.logical_and(i == n_blocks - 1, f == nf - 1), i < n_used))
    def _():
        wait_block(i)

    @pl.when(jnp.logical_and(i >= n_used, f == 0))
    def _():
        y_ref[...] = jnp.zeros(y_ref.shape, y_ref.dtype)


def _moe_blocks_max(n_tok):
    m = n_tok * TOP_K
    return -(-(m + N_EXP * (MOE_BM - 1)) // MOE_BM)


def _moe_experts(h, row_tok, block_expert, block_valid, n_used, layer, w1, b1, w2, b2, sel):
    bm, fh = MOE_BM, MOE_FH
    n_blocks = row_tok.shape[0] // bm
    nf = D_EXP // fh
    e0 = layer * N_EXP

    def blk(i, nu):
        return jnp.minimum(i, nu[0] - 1)

    grid_spec = pltpu.PrefetchScalarGridSpec(
        num_scalar_prefetch=4,
        grid=(n_blocks, nf),
        in_specs=[pl.BlockSpec(memory_space=pl.ANY),
                  pl.BlockSpec((1, D, 2 * fh), lambda i, f, be, nu, tok, nv: (e0 + be[blk(i, nu)], 0, f)),
                  pl.BlockSpec((1, 1, 2 * fh), lambda i, f, be, nu, tok, nv: (be[blk(i, nu)], 0, f)),
                  pl.BlockSpec((1, fh, D), lambda i, f, be, nu, tok, nv: (e0 + be[blk(i, nu)], f, 0)),
                  pl.BlockSpec((1, 1, D), lambda i, f, be, nu, tok, nv: (be[blk(i, nu)], 0, 0)),
                  pl.BlockSpec((2 * fh, fh), lambda i, f, be, nu, tok, nv: (0, 0))],
        out_specs=pl.BlockSpec((bm, D), lambda i, f, be, nu, tok, nv: (i, 0)),
        scratch_shapes=[pltpu.VMEM((bm, D), F32), pltpu.VMEM((bm, D), BF16), pltpu.SemaphoreType.DMA],
    )
    return pl.pallas_call(
        _moe_kernel,
        out_shape=jax.ShapeDtypeStruct((n_blocks * bm, D), F32),
        grid_spec=grid_spec,
        compiler_params=pltpu.CompilerParams(dimension_semantics=("arbitrary", "arbitrary")),
        name="moe_experts",
    )(block_expert, n_used, row_tok, block_valid, h, w1, b1, w2, b2, sel)


def _combine_kernel(dest_ref, x_ref, y_hbm, p_ref, gate_ref, o_ref, buf, sem):
    i = pl.program_id(0)
    tm = x_ref.shape[0]

    def row_copy(kk, r):
        d = dest_ref[(i * tm + r) * TOP_K + kk]
        return pltpu.make_async_copy(y_hbm.at[pl.ds(d, 1)], buf.at[kk, pl.ds(r, 1)], sem.at[kk])

    for kk in range(TOP_K):
        def start(r8, carry, kk=kk):
            for j in range(8):
                row_copy(kk, r8 * 8 + j).start(priority=j % 2)
            return carry
        lax.fori_loop(0, tm // 8, start, 0)

    p = p_ref[...]
    acc = None
    for kk in range(TOP_K):
        def wait(r, carry, kk=kk):
            row_copy(kk, r).wait()
            return carry
        lax.fori_loop(0, tm, wait, 0, unroll=8)
        term = p[:, kk:kk + 1] * buf[kk]
        acc = term if acc is None else acc + term
    o_ref[...] = x_ref[...] + gate_ref[0] * acc


def _moe_combine(xs, y_rows, dest, probs, mod, gate_chunk, n_rows):
    tm = 256
    grid_spec = pltpu.PrefetchScalarGridSpec(
        num_scalar_prefetch=1,
        grid=(n_rows // tm,),
        in_specs=[pl.BlockSpec((tm, D), lambda i, dst: (i, 0)),
                  pl.BlockSpec(memory_space=pl.ANY),
                  pl.BlockSpec((tm, LANES), lambda i, dst: (i, 0)),
                  pl.BlockSpec((1, 1, D), lambda i, dst: (_mod_row(i, tm), 0, gate_chunk))],
        out_specs=pl.BlockSpec((tm, D), lambda i, dst: (i, 0)),
        scratch_shapes=[pltpu.VMEM((TOP_K, tm, D), F32), pltpu.SemaphoreType.DMA((TOP_K,))],
    )
    return pl.pallas_call(
        _combine_kernel,
        out_shape=jax.ShapeDtypeStruct((n_rows, D), F32),
        grid_spec=grid_spec,
        compiler_params=pltpu.CompilerParams(dimension_semantics=("arbitrary",)),
        name="moe_combine",
    )(dest, xs, y_rows, probs, mod)


def _moe_route(top_e, n_tok):
    m = n_tok * TOP_K
    bm = MOE_BM
    n_blocks = _moe_blocks_max(n_tok)
    e_flat = top_e.reshape(m)
    onehot = (e_flat[:, None] == jnp.arange(N_EXP, dtype=jnp.int32)[None, :]).astype(jnp.int32)
    csum = jnp.cumsum(onehot, axis=0)
    counts = csum[-1]
    rank = jnp.take_along_axis(csum, e_flat[:, None], axis=1)[:, 0] - 1
    padded = (counts + bm - 1) // bm * bm
    pad_end = jnp.cumsum(padded)
    pad_start = pad_end - padded
    dest = pad_start[e_flat] + rank
    n_used = (pad_end[-1] // bm).astype(jnp.int32).reshape(1)
    starts = jnp.arange(n_blocks, dtype=jnp.int32) * bm
    block_expert = jnp.minimum(jnp.sum((pad_end[None, :] <= starts[:, None]).astype(jnp.int32), axis=1),
                               N_EXP - 1).astype(jnp.int32)
    order = jnp.argsort(e_flat, stable=True).astype(jnp.int32)
    sort_start = jnp.cumsum(counts) - counts
    rows = jnp.arange(n_blocks * bm, dtype=jnp.int32)
    shift = jnp.repeat((sort_start - pad_start)[block_expert], bm)
    limit = jnp.repeat((pad_start + counts)[block_expert], bm)
    row_tok = jnp.where(rows < limit, order[jnp.clip(rows + shift, 0, m - 1)] // TOP_K, 0)
    block_valid = jnp.clip((pad_start + counts)[block_expert] - starts, 0, bm).astype(jnp.int32)
    return dest, row_tok, block_expert, block_valid, n_used


def _moe_layer(xs, n_rows, g, mod, router_w, router_b, layer, w1_all, b1, w2_all, b2, sel, name):
    rw = jnp.pad(router_w, ((0, 0), (0, LANES - N_EXP)))
    rb = jnp.pad(router_b, (0, LANES - N_EXP), constant_values=NEG).reshape(1, LANES)
    h, top_e, probs = _norm_mod(xs, g, mod, 3, 4, n_rows=n_rows, router_w=rw, router_b=rb, name=name + "_norm")
    dest, row_tok, block_expert, block_valid, n_used = _moe_route(top_e[:, :TOP_K], n_rows)
    y_rows = _moe_experts(h, row_tok, block_expert, block_valid, n_used, layer, w1_all,
                          b1.reshape(N_EXP, 1, 2 * D_EXP), w2_all, b2.reshape(N_EXP, 1, D), sel)
    return _moe_combine(xs, y_rows, dest, probs, mod, 5, n_rows)


def _select_matrix():
    r = np.arange(2 * MOE_FH)[:, None]
    c = np.arange(MOE_FH)[None, :]
    return jnp.asarray(r == 2 * c, BF16)


def kernel(x, c, ctx, c_ctx, ada_w, ada_b, norm_g, attn_w_in, attn_w_out, attn_qk_gain, na_rpb, mlstm_w_in,
           mlstm_gate_bias, mlstm_head_gain, mlstm_w_out, router_w, router_b, expert_w1, expert_b1,
           expert_w2, expert_b2):
    xs = jnp.concatenate([x.reshape(T_LAT, D), ctx.reshape(T_CTX, D)], axis=0)
    cc = jnp.zeros((8, D), F32).at[:B].set(c).at[MOD_CTX_ROW].set(c_ctx)
    sel = _select_matrix()

    n_layers = ada_w.shape[0]
    ada_w2 = ada_w.reshape(n_layers * D, 6 * D)
    w1_all = expert_w1.reshape(n_layers * N_EXP, D, 2 * D_EXP)
    w2_all = expert_w2.reshape(n_layers * N_EXP, D_EXP, D)

    def ada(layer):
        m = _matmul(cc, ada_w2, 6 * D, out_dtype=F32, tm=8, w_row_block=layer, silu=True,
                    bias=ada_b[layer].reshape(1, 6 * D), name="adaln")
        return m.reshape(8, 1, 6 * D)

    mod = ada(0)
    h = _norm_mod(xs, norm_g[0, 0].reshape(1, D), mod, 0, 1, n_rows=T_ALL, name="l0_norm1")
    qkv = _matmul(h, attn_w_in[0], ATTN_IN, out_dtype=BF16, name="attn_in")
    cos_rep, sin_signed = _rope_tables()
    qkv = _qk_prep(qkv, attn_qk_gain[0].reshape(4, 1, HD), cos_rep, sin_signed)
    o_buf = _na_lat(qkv, _na_bias_table(na_rpb[0]), jnp.zeros((T_ALL, D), BF16))
    o_buf = _gqa_lat(qkv, o_buf)
    o_buf = _ctx_attn(qkv, o_buf)
    xs = _matmul(o_buf, attn_w_out[0], D, out_dtype=F32, res=xs, mod=mod, gate_chunk=2, name="attn_out")
    xs = _moe_layer(xs, T_ALL, norm_g[0, 1].reshape(1, D), mod, router_w[0], router_b[0],
                    0, w1_all, expert_b1[0], w2_all, expert_b2[0], sel, "l0_moe")

    mod = ada(1)
    h = _norm_mod(xs, norm_g[1, 0].reshape(1, D), mod, 0, 1, n_rows=T_ALL, name="l1_norm1")
    proj = _matmul(h, mlstm_w_in[0], ML_MAIN, out_dtype=BF16, name="mlstm_in")
    wg = jnp.pad(mlstm_w_in[0][:, ML_MAIN:], ((0, 0), (0, LANES - 4 * ML_H)))
    bg = jnp.pad(mlstm_gate_bias[0].reshape(-1), (0, LANES - 4 * ML_H)).reshape(1, LANES)
    g = _mlstm_gates(h, wg, bg)
    gates = g[:, :4 * ML_H].T.reshape(4, ML_H, T_ALL).transpose(1, 0, 2)
    hf, hb = _mlstm_scan(proj, gates)
    hn = _mlstm_readout(hf, hb, proj, mlstm_head_gain[0].reshape(1, ML_H * ML_DV), T_LAT)
    xs = _matmul(hn, mlstm_w_out[0], D, out_dtype=F32, m_rows=T_LAT, res=xs, mod=mod, gate_chunk=2,
                 name="mlstm_out")
    out = _moe_layer(xs, T_LAT, norm_g[1, 1].reshape(1, D), mod, router_w[1], router_b[1],
                     1, w1_all, expert_b1[1], w2_all, expert_b2[1], sel, "l1_moe")
    return out.reshape(B, S, D)
```
